```python
import math
import jax, jax.numpy as jnp
from jax import lax
import numpy as np

D_MODEL = 2048
BATCH = 4
SEQ = 2048
DEPTH = 1

CHUNK = 64
Q_BLOCK = 128
SB_HEADS = 8
SB_HEAD_DIM = 128
D_SB = SB_HEADS * SB_HEAD_DIM
D_CONV = D_MODEL // 2
CONV_WIDTH = 31
D_FF = 4 * D_MODEL
EPS = 1e-6
IN_SPLITS = (D_SB, D_SB, D_SB, D_CONV, D_CONV, D_MODEL, D_MODEL)
D_IN = sum(IN_SPLITS)

kernel_name = "stickbreak_conformer_gated_hybrid"


def rmsnorm(x, g):
    xf = x.astype(jnp.float32)
    y = xf * lax.rsqrt(jnp.mean(xf * xf, axis=-1, keepdims=True) + EPS)
    return (y * g.astype(jnp.float32)).astype(x.dtype)


def layernorm(x, g, b):
    xf = x.astype(jnp.float32)
    mu = jnp.mean(xf, axis=-1, keepdims=True)
    var = jnp.mean(jnp.square(xf - mu), axis=-1, keepdims=True)
    y = (xf - mu) * lax.rsqrt(var + EPS)
    return (y * g.astype(jnp.float32) + b.astype(jnp.float32)).astype(x.dtype)


def stick_breaking_attention(q, k, v):
    b, s, h, dh = q.shape
    scale = 1.0 / math.sqrt(dh)
    outs = []
    for i in range(s // Q_BLOCK):
        t0, t1 = i * Q_BLOCK, (i + 1) * Q_BLOCK
        qb = q[:, t0:t1]
        kb = k[:, :t1]
        vb = v[:, :t1]
        z = jnp.einsum('bqhd,bkhd->bhqk', qb, kb).astype(jnp.float32) * scale
        tq = t0 + jnp.arange(Q_BLOCK)[:, None]
        sk = jnp.arange(t1)[None, :]
        mask = sk < tq
        log_1m = jnp.where(mask, jax.nn.log_sigmoid(-z), 0.0)
        suffix = lax.cumsum(log_1m, axis=3, reverse=True) - log_1m
        a = jnp.where(mask, jnp.exp(jax.nn.log_sigmoid(z) + suffix), 0.0)
        outs.append(jnp.einsum('bhqk,bkhd->bqhd', a.astype(vb.dtype), vb))
    return jnp.concatenate(outs, axis=1)


def causal_depthwise_conv(u, w, bias):
    c = u.shape[-1]
    y = lax.conv_general_dilated(
        u, w.reshape(CONV_WIDTH, 1, c).astype(u.dtype),
        window_strides=(1,), padding=[(CONV_WIDTH - 1, 0)],
        dimension_numbers=('NWC', 'WIO', 'NWC'), feature_group_count=c)
    return y + bias


def setup_inputs(seed: int = 0) -> dict:
    key = jax.random.key(seed)
    ks = jax.random.split(key, 17)
    L = DEPTH
    nrm = lambda k, shape, fan_in: jax.random.normal(k, shape, jnp.float32) * (fan_in ** -0.5)
    gain = lambda k, shape: 1.0 + 0.02 * jax.random.normal(k, shape, jnp.float32)
    small = lambda k, shape: 0.02 * jax.random.normal(k, shape, jnp.float32)
    return {
        "x": jax.random.normal(ks[0], (BATCH, SEQ, D_MODEL), jnp.float32),
        "g_pre_mix": gain(ks[1], (L, D_MODEL)),
        "w_in": nrm(ks[2], (L, D_MODEL, D_IN), D_MODEL),
        "b_in": small(ks[3], (L, D_IN)),
        "w_dw": nrm(ks[4], (L, CONV_WIDTH, D_CONV), CONV_WIDTH),
        "b_dw": small(ks[5], (L, D_CONV)),
        "g_conv_ln": gain(ks[6], (L, D_CONV)),
        "b_conv_ln": small(ks[7], (L, D_CONV)),
        "w_sb_out": nrm(ks[8], (L, D_SB, D_MODEL), D_SB),
        "w_conv_out": nrm(ks[9], (L, D_CONV, D_MODEL), D_CONV),
        "w_o": nrm(ks[10], (L, D_MODEL, D_MODEL), D_MODEL),
        "g_post_mix": gain(ks[11], (L, D_MODEL)),
        "g_pre_mlp": gain(ks[12], (L, D_MODEL)),
        "w_up": nrm(ks[13], (L, D_MODEL, D_FF), D_MODEL),
        "w_down": nrm(ks[14], (L, D_FF, D_MODEL), D_FF),
        "g_post_mlp": gain(ks[15], (L, D_MODEL)),
    }


def reference(x, g_pre_mix, w_in, b_in, w_dw, b_dw, g_conv_ln, b_conv_ln,
              w_sb_out, w_conv_out, w_o, g_post_mix, g_pre_mlp, w_up, w_down,
              g_post_mlp):
    b, s, _ = x.shape
    offs = np.cumsum(IN_SPLITS)[:-1].tolist()
    for l in range(DEPTH):
        h = rmsnorm(x, g_pre_mix[l])
        proj = jnp.einsum('bsd,de->bse', h, w_in[l]) + b_in[l]
        q, k, v, glu_a, glu_b, gate_sb, gate_cv = jnp.split(proj, offs, axis=-1)
        hd = (b, s, SB_HEADS, SB_HEAD_DIM)
        o_sb = stick_breaking_attention(q.reshape(hd), k.reshape(hd), v.reshape(hd))
        o_sb = jnp.einsum('bse,ed->bsd', o_sb.reshape(b, s, D_SB), w_sb_out[l])
        u = glu_a * jax.nn.sigmoid(glu_b)
        u = causal_depthwise_conv(u, w_dw[l], b_dw[l])
        u = jax.nn.silu(layernorm(u, g_conv_ln[l], b_conv_ln[l]))
        o_cv = jnp.einsum('bsc,cd->bsd', u, w_conv_out[l])
        merged = jax.nn.sigmoid(gate_sb) * o_sb + jax.nn.sigmoid(gate_cv) * o_cv
        y = jnp.einsum('bsd,de->bse', merged, w_o[l])
        x = x + rmsnorm(y, g_post_mix[l])
        h = rmsnorm(x, g_pre_mlp[l])
        f = jnp.square(jax.nn.relu(jnp.einsum('bsd,df->bsf', h, w_up[l])))
        f = jnp.einsum('bsf,fd->bsd', f, w_down[l])
        x = x + rmsnorm(f, g_post_mlp[l])
    return x
```

```python
import functools
import math

import jax
import jax.numpy as jnp
from jax import lax
from jax.experimental import pallas as pl
from jax.experimental.pallas import tpu as pltpu

F32 = jnp.float32
BF16 = jnp.bfloat16

EPS = 1e-6
HEAD_DIM = 128
CONV_WIDTH = 31
CONV_HALO = 32

VMEM_LIMIT_BYTES = 56 * 1024 * 1024

INPROJ_TM = 1024
INPROJ_TN = 1024
ATTN_TQ = 256
ATTN_SUB = 128
ATTN_HEADS_PER_STEP = 2
CONV_TS = 256
CONV_ROWS = 32
MERGE_TM = 512
MLP_TM = 512
MLP_TF = 1024


def _params(*sem):
    return pltpu.CompilerParams(dimension_semantics=sem, vmem_limit_bytes=VMEM_LIMIT_BYTES)


def _rms_scale(x):
    return lax.rsqrt(jnp.mean(x * x, axis=-1, keepdims=True) + EPS)


def _inproj_kernel(x_ref, g_ref, w_ref, b_ref, qkv_ref, u_ref, gates_ref, h_ref, a_ref):
    j = pl.program_id(1)

    @pl.when(j == 0)
    def _():
        x = x_ref[...]
        h_ref[...] = (x * _rms_scale(x) * g_ref[...]).astype(BF16)

    def proj():
        return jnp.dot(h_ref[...], w_ref[...], preferred_element_type=F32) + b_ref[...]

    @pl.when(j < 3)
    def _():
        qkv_ref[...] = proj().astype(BF16)

    @pl.when(j == 3)
    def _():
        a_ref[...] = proj()

    @pl.when(j == 4)
    def _():
        u_ref[...] = a_ref[...] * jax.nn.sigmoid(proj())

    @pl.when(j >= 5)
    def _():
        gates_ref[...] = jax.nn.sigmoid(proj()).astype(BF16)


def _inproj(x2, g, w, b):
    m, d = x2.shape
    tm, tn = INPROJ_TM, INPROJ_TN
    n_tiles = w.shape[1] // tn
    assert n_tiles == 9 and tn == 1024
    return pl.pallas_call(
        _inproj_kernel,
        grid=(m // tm, n_tiles),
        in_specs=[
            pl.BlockSpec((tm, d), lambda i, j: (i, 0)),
            pl.BlockSpec((1, d), lambda i, j: (0, 0)),
            pl.BlockSpec((d, tn), lambda i, j: (0, j)),
            pl.BlockSpec((1, tn), lambda i, j: (0, j)),
        ],
        out_specs=[
            pl.BlockSpec((tm, tn), lambda i, j: (i, jnp.minimum(j, 2))),
            pl.BlockSpec((tm, tn), lambda i, j: (i, 0)),
            pl.BlockSpec((tm, tn), lambda i, j: (i, jnp.clip(j - 5, 0, 3))),
        ],
        out_shape=[
            jax.ShapeDtypeStruct((m, 3 * tn), BF16),
            jax.ShapeDtypeStruct((m, tn), F32),
            jax.ShapeDtypeStruct((m, 4 * tn), BF16),
        ],
        scratch_shapes=[pltpu.VMEM((tm, d), BF16), pltpu.VMEM((tm, tn), F32)],
        compiler_params=_params("parallel", "arbitrary"),
        name="inproj",
    )(x2, g, w, b)


def _attn_kernel(q_ref, k_ref, v_ref, tri_ref, o_ref, *, heads, tq, sub):
    qi = pl.program_id(2)
    scale = 1.0 / math.sqrt(HEAD_DIM)
    nsub = tq // sub
    tri = tri_ref[...]
    row = lax.broadcasted_iota(jnp.int32, (tq, tq), 0)
    col = lax.broadcasted_iota(jnp.int32, (tq, tq), 1)
    causal = col < row

    def tile(h, k0, carry, acc, diag):
        lanes = pl.ds(h * HEAD_DIM, HEAD_DIM)
        q = q_ref[0, :, lanes]
        kb = k_ref[0, pl.ds(k0, tq), lanes]
        vb = v_ref[0, pl.ds(k0, tq), lanes]
        z = lax.dot_general(q, kb, (((1,), (1,)), ((), ())), preferred_element_type=F32) * scale
        log1m = -(jnp.maximum(z, 0.0) + jnp.log(1.0 + jnp.exp(-jnp.abs(z))))
        if diag:
            log1m = jnp.where(causal, log1m, 0.0)
        hi = log1m.astype(BF16)
        lo = (log1m - hi.astype(F32)).astype(BF16)
        lhs = jnp.concatenate(
            [jnp.concatenate([hi[:, s * sub:(s + 1) * sub], lo[:, s * sub:(s + 1) * sub]], axis=1)
             for s in range(nsub)], axis=0)
        cs = jnp.dot(lhs, tri, preferred_element_type=F32)
        parts = [None] * nsub
        for s in reversed(range(nsub)):
            blk = cs[s * tq:(s + 1) * tq]
            parts[s] = jnp.exp(z[:, s * sub:(s + 1) * sub] + blk[:, :sub] + carry)
            carry = carry + blk[:, sub:]
        a = jnp.concatenate(parts, axis=1)
        if diag:
            a = jnp.where(causal, a, 0.0)
        acc = acc + jnp.dot(a.astype(BF16), vb, preferred_element_type=F32)
        return carry, acc

    zero = jnp.zeros((tq, HEAD_DIM), F32)
    state = []
    for h in range(heads):
        state.extend(tile(h, pl.multiple_of(qi * tq, tq), zero, zero, True))

    def body(it, st):
        k0 = pl.multiple_of((qi - 1 - it) * tq, tq)
        new = []
        for h in range(heads):
            new.extend(tile(h, k0, st[2 * h], st[2 * h + 1], False))
        return tuple(new)

    state = lax.fori_loop(0, qi, body, tuple(state))
    for h in range(heads):
        o_ref[0, :, pl.ds(h * HEAD_DIM, HEAD_DIM)] = state[2 * h + 1].astype(o_ref.dtype)


def _attention(qkv3, n_heads):
    b, s, _ = qkv3.shape
    tq, sub, heads = ATTN_TQ, ATTN_SUB, ATTN_HEADS_PER_STEP
    width = heads * HEAD_DIM
    groups = n_heads // heads
    r = jnp.arange(2 * sub) % sub
    c = jnp.arange(2 * sub)
    tri = jnp.where(c[None, :] < sub, r[:, None] >= c[None, :], True).astype(BF16)
    kern = functools.partial(_attn_kernel, heads=heads, tq=tq, sub=sub)
    return pl.pallas_call(
        kern,
        grid=(b, groups, s // tq),
        in_specs=[
            pl.BlockSpec((1, tq, width), lambda bi, g, qi: (bi, qi, g)),
            pl.BlockSpec((1, s, width), lambda bi, g, qi: (bi, 0, groups + g)),
            pl.BlockSpec((1, s, width), lambda bi, g, qi: (bi, 0, 2 * groups + g)),
            pl.BlockSpec((2 * sub, 2 * sub), lambda bi, g, qi: (0, 0)),
        ],
        out_specs=pl.BlockSpec((1, tq, width), lambda bi, g, qi: (bi, qi, g)),
        out_shape=jax.ShapeDtypeStruct((b, s, n_heads * HEAD_DIM), BF16),
        compiler_params=_params("parallel", "parallel", "arbitrary"),
        name="attention",
    )(qkv3, qkv3, qkv3, tri)


def _conv_kernel(u_ref, halo_ref, w_ref, bdw_ref, g_ref, bln_ref, o_ref, xs_ref, *, ts, rows):
    i = pl.program_id(1)
    xs_ref[0:CONV_HALO, :] = jnp.where(i > 0, halo_ref[0], 0.0)
    xs_ref[CONV_HALO:, :] = u_ref[0]
    first = CONV_HALO - (CONV_WIDTH - 1)
    for r0 in range(0, ts, rows):
        acc = xs_ref[pl.ds(first + r0, rows), :] * w_ref[0:1, :]
        for j in range(1, CONV_WIDTH):
            acc = acc + xs_ref[pl.ds(first + r0 + j, rows), :] * w_ref[j:j + 1, :]
        acc = acc + bdw_ref[...]
        mu = jnp.mean(acc, axis=-1, keepdims=True)
        cen = acc - mu
        var = jnp.mean(cen * cen, axis=-1, keepdims=True)
        y = cen * lax.rsqrt(var + EPS) * g_ref[...] + bln_ref[...]
        o_ref[0, pl.ds(r0, rows), :] = (y * jax.nn.sigmoid(y)).astype(o_ref.dtype)


def _conv_branch(u3, w_dw, b_dw, g_ln, b_ln):
    b, s, c = u3.shape
    ts, rows = CONV_TS, CONV_ROWS
    ratio = ts // CONV_HALO
    kern = functools.partial(_conv_kernel, ts=ts, rows=rows)
    vec = pl.BlockSpec((1, c), lambda bi, i: (0, 0))
    return pl.pallas_call(
        kern,
        grid=(b, s // ts),
        in_specs=[
            pl.BlockSpec((1, ts, c), lambda bi, i: (bi, i, 0)),
            pl.BlockSpec((1, CONV_HALO, c), lambda bi, i: (bi, jnp.maximum(i * ratio - 1, 0), 0)),
            pl.BlockSpec((CONV_WIDTH, c), lambda bi, i: (0, 0)),
            vec, vec, vec,
        ],
        out_specs=pl.BlockSpec((1, ts, c), lambda bi, i: (bi, i, 0)),
        out_shape=jax.ShapeDtypeStruct((b, s, c), BF16),
        scratch_shapes=[pltpu.VMEM((ts + CONV_HALO, c), F32)],
        compiler_params=_params("parallel", "arbitrary"),
        name="conv",
    )(u3, u3, w_dw, b_dw, g_ln, b_ln)


def _merge_kernel(x_ref, osb_ref, ocv_ref, gsb_ref, gcv_ref, wsb_ref, wcv_ref, wo_ref, g_ref, o_ref):
    a = jnp.dot(osb_ref[...], wsb_ref[...], preferred_element_type=F32)
    c = jnp.dot(ocv_ref[...], wcv_ref[...], preferred_element_type=F32)
    merged = gsb_ref[...].astype(F32) * a + gcv_ref[...].astype(F32) * c
    y = jnp.dot(merged.astype(BF16), wo_ref[...], preferred_element_type=F32)
    o_ref[...] = x_ref[...] + y * _rms_scale(y) * g_ref[...]


def _merge(x2, osb, ocv, gates, w_sb, w_cv, w_o, g):
    m, d = x2.shape
    dh = osb.shape[1]
    tm = MERGE_TM
    resident = lambda shape: pl.BlockSpec(shape, lambda i: (0, 0), pipeline_mode=pl.Buffered(1))
    return pl.pallas_call(
        _merge_kernel,
        grid=(m // tm,),
        in_specs=[
            pl.BlockSpec((tm, d), lambda i: (i, 0)),
            pl.BlockSpec((tm, dh), lambda i: (i, 0)),
            pl.BlockSpec((tm, dh), lambda i: (i, 0)),
            pl.BlockSpec((tm, d), lambda i: (i, 0)),
            pl.BlockSpec((tm, d), lambda i: (i, 1)),
            resident((dh, d)), resident((dh, d)), resident((d, d)), resident((1, d)),
        ],
        out_specs=pl.BlockSpec((tm, d), lambda i: (i, 0)),
        out_shape=jax.ShapeDtypeStruct((m, d), F32),
        compiler_params=_params("parallel"),
        name="merge",
    )(x2, osb, ocv, gates, gates, w_sb, w_cv, w_o, g)


def _mlp_kernel(x_ref, g1_ref, wu_ref, wd_ref, g2_ref, o_ref, h_ref, acc_ref):
    f = pl.program_id(1)

    @pl.when(f == 0)
    def _():
        x = x_ref[...]
        h_ref[...] = (x * _rms_scale(x) * g1_ref[...]).astype(BF16)

    t = jnp.dot(h_ref[...], wu_ref[...], preferred_element_type=F32)
    t = jnp.square(jnp.maximum(t, 0.0)).astype(BF16)
    d = jnp.dot(t, wd_ref[...], preferred_element_type=F32)

    @pl.when(f == 0)
    def _():
        acc_ref[...] = d

    @pl.when(f > 0)
    def _():
        acc_ref[...] += d

    @pl.when(f == pl.num_programs(1) - 1)
    def _():
        y = acc_ref[...]
        o_ref[...] = x_ref[...] + y * _rms_scale(y) * g2_ref[...]


def _mlp(x2, g1, w_up, w_down, g2):
    m, d = x2.shape
    dff = w_up.shape[1]
    tm, tf = MLP_TM, MLP_TF
    vec = pl.BlockSpec((1, d), lambda i, f: (0, 0))
    return pl.pallas_call(
        _mlp_kernel,
        grid=(m // tm, dff // tf),
        in_specs=[
            pl.BlockSpec((tm, d), lambda i, f: (i, 0)),
            vec,
            pl.BlockSpec((d, tf), lambda i, f: (0, f)),
            pl.BlockSpec((tf, d), lambda i, f: (f, 0)),
            vec,
        ],
        out_specs=pl.BlockSpec((tm, d), lambda i, f: (i, 0)),
        out_shape=jax.ShapeDtypeStruct((m, d), F32),
        scratch_shapes=[pltpu.VMEM((tm, d), BF16), pltpu.VMEM((tm, d), F32)],
        compiler_params=_params("parallel", "arbitrary"),
        name="mlp",
    )(x2, g1, w_up, w_down, g2)


def kernel(x, g_pre_mix, w_in, b_in, w_dw, b_dw, g_conv_ln, b_conv_ln, w_sb_out, w_conv_out, w_o,
           g_post_mix, g_pre_mlp, w_up, w_down, g_post_mlp):
    b, s, d = x.shape
    d_sb = w_sb_out.shape[1]
    d_conv = w_dw.shape[2]
    n_heads = d_sb // HEAD_DIM
    row = lambda v: v.reshape(1, -1)
    x2 = x.reshape(b * s, d)
    for l in range(w_in.shape[0]):
        qkv, u, gates = _inproj(x2, row(g_pre_mix[l]), w_in[l].astype(BF16), row(b_in[l]))
        o_sb = _attention(qkv.reshape(b, s, 3 * d_sb), n_heads)
        o_cv = _conv_branch(u.reshape(b, s, d_conv), w_dw[l], row(b_dw[l]), row(g_conv_ln[l]),
                            row(b_conv_ln[l]))
        x2 = _merge(x2, o_sb.reshape(b * s, d_sb), o_cv.reshape(b * s, d_conv), gates,
                    w_sb_out[l].astype(BF16), w_conv_out[l].astype(BF16), w_o[l].astype(BF16),
                    row(g_post_mix[l]))
        x2 = _mlp(x2, row(g_pre_mlp[l]), w_up[l].astype(BF16), w_down[l].astype(BF16),
                  row(g_post_mlp[l]))
    return x2.reshape(b, s, d)
```

```python
import functools
import math

import jax
import jax.numpy as jnp
from jax import lax
from jax.experimental import pallas as pl
from jax.experimental.pallas import tpu as pltpu

F32 = jnp.float32
BF16 = jnp.bfloat16

EPS = 1e-6
LOG2E = 1.4426950408889634
HEAD_DIM = 128
CONV_WIDTH = 31
CONV_HALO = 32

VMEM_LIMIT_BYTES = 56 * 1024 * 1024

INPROJ_TM = 1024
INPROJ_TN = 1024
ATTN_TQ = 256
ATTN_SUB = 128
ATTN_HEADS_PER_STEP = 4
CONV_TS = 256
CONV_ROWS = 32
MERGE_TM = 512
MLP_TM = 512
MLP_TF = 1024


def _params(*sem):
    return pltpu.CompilerParams(dimension_semantics=sem, vmem_limit_bytes=VMEM_LIMIT_BYTES)


def _rms_scale(x):
    return lax.rsqrt(jnp.mean(x * x, axis=-1, keepdims=True) + EPS)


def _inproj_kernel(x_ref, g_ref, w_ref, b_ref, qkv_ref, u_ref, gates_ref, h_ref, a_ref):
    j = pl.program_id(1)

    @pl.when(j == 0)
    def _():
        x = x_ref[...]
        h_ref[...] = (x * _rms_scale(x) * g_ref[...]).astype(BF16)

    def proj():
        return jnp.dot(h_ref[...], w_ref[...], preferred_element_type=F32) + b_ref[...]

    @pl.when(j < 3)
    def _():
        qkv_ref[...] = proj().astype(BF16)

    @pl.when(j == 3)
    def _():
        a_ref[...] = proj()

    @pl.when(j == 4)
    def _():
        u_ref[...] = a_ref[...] * jax.nn.sigmoid(proj())

    @pl.when(j >= 5)
    def _():
        gates_ref[...] = jax.nn.sigmoid(proj()).astype(BF16)


def _inproj(x2, g, w, b):
    m, d = x2.shape
    tm, tn = INPROJ_TM, INPROJ_TN
    n_tiles = w.shape[1] // tn
    assert n_tiles == 9 and tn == 1024
    return pl.pallas_call(
        _inproj_kernel,
        grid=(m // tm, n_tiles),
        in_specs=[
            pl.BlockSpec((tm, d), lambda i, j: (i, 0)),
            pl.BlockSpec((1, d), lambda i, j: (0, 0)),
            pl.BlockSpec((d, tn), lambda i, j: (0, j)),
            pl.BlockSpec((1, tn), lambda i, j: (0, j)),
        ],
        out_specs=[
            pl.BlockSpec((tm, tn), lambda i, j: (i, jnp.minimum(j, 2))),
            pl.BlockSpec((tm, tn), lambda i, j: (i, 0)),
            pl.BlockSpec((tm, tn), lambda i, j: (i, jnp.clip(j - 5, 0, 3))),
        ],
        out_shape=[
            jax.ShapeDtypeStruct((m, 3 * tn), BF16),
            jax.ShapeDtypeStruct((m, tn), F32),
            jax.ShapeDtypeStruct((m, 4 * tn), BF16),
        ],
        scratch_shapes=[pltpu.VMEM((tm, d), BF16), pltpu.VMEM((tm, tn), F32)],
        compiler_params=_params("parallel", "arbitrary"),
        name="inproj",
    )(x2, g, w, b)


def _attn_kernel(q_ref, k_ref, v_ref, tri_ref, o_ref, z_a, z_b, l_a, l_b, carry_ref, acc_ref,
                 *, heads, tq, sub):
    qi = pl.program_id(2)
    scale = 1.0 / math.sqrt(HEAD_DIM)
    nsub = tq // sub
    row = lax.broadcasted_iota(jnp.int32, (tq, tq), 0)
    col = lax.broadcasted_iota(jnp.int32, (tq, tq), 1)
    causal = col < row

    def key_start(m):
        return pl.multiple_of((qi - 1 - m) * tq, tq)

    def scores(k0, z_buf, l_buf, diag=False):
        for h in range(heads):
            lanes = pl.ds(h * HEAD_DIM, HEAD_DIM)
            kb = k_ref[0, pl.ds(k0, tq), lanes]
            z = lax.dot_general(q_ref[0, :, lanes], kb, (((1,), (1,)), ((), ())),
                                preferred_element_type=F32) * scale
            z_buf[h] = z
            softplus = jnp.maximum(z, 0.0) + jnp.log(1.0 + jnp.exp2(jnp.abs(z) * (-LOG2E)))
            if diag:
                softplus = jnp.where(causal, softplus, 0.0)
            hi = softplus.astype(BF16)
            lo = (softplus - hi.astype(F32)).astype(BF16)
            for s in range(nsub):
                l_buf[h, s * tq:(s + 1) * tq, 0:sub] = hi[:, s * sub:(s + 1) * sub]
                l_buf[h, s * tq:(s + 1) * tq, sub:2 * sub] = lo[:, s * sub:(s + 1) * sub]

    def finish(k0, z_buf, l_buf, diag=False):
        css = [jnp.dot(l_buf[h], tri_ref[...], preferred_element_type=F32) for h in range(heads)]
        for h in range(heads):
            carry = jnp.zeros((tq, HEAD_DIM), F32) if diag else carry_ref[h]
            parts = [None] * nsub
            for s in reversed(range(nsub)):
                blk = css[h][s * tq:(s + 1) * tq]
                parts[s] = jnp.exp(z_buf[h, :, s * sub:(s + 1) * sub] + blk[:, :sub] + carry)
                carry = carry + blk[:, sub:]
            carry_ref[h] = carry
            a = jnp.concatenate(parts, axis=1)
            if diag:
                a = jnp.where(causal, a, 0.0)
            vb = v_ref[0, pl.ds(k0, tq), pl.ds(h * HEAD_DIM, HEAD_DIM)]
            av = jnp.dot(a.astype(BF16), vb, preferred_element_type=F32)
            if diag:
                acc_ref[h] = av
            else:
                acc_ref[h] += av

    k_diag = pl.multiple_of(qi * tq, tq)
    scores(k_diag, z_a, l_a, diag=True)
    finish(k_diag, z_a, l_a, diag=True)

    @pl.when(qi > 0)
    def _():
        scores(key_start(0), z_a, l_a)

        def pair(p, _):
            m = 2 * p
            scores(key_start(m + 1), z_b, l_b)
            finish(key_start(m), z_a, l_a)
            scores(key_start(m + 2), z_a, l_a)
            finish(key_start(m + 1), z_b, l_b)
            return 0

        n_pairs = (qi - 1) // 2
        lax.fori_loop(0, n_pairs, pair, 0)
        m = 2 * n_pairs

        @pl.when(qi - m == 1)
        def _():
            finish(key_start(m), z_a, l_a)

        @pl.when(qi - m == 2)
        def _():
            scores(key_start(m + 1), z_b, l_b)
            finish(key_start(m), z_a, l_a)
            finish(key_start(m + 1), z_b, l_b)

    for h in range(heads):
        o_ref[0, :, pl.ds(h * HEAD_DIM, HEAD_DIM)] = acc_ref[h].astype(o_ref.dtype)


def _attention(qkv3, n_heads):
    b, s, _ = qkv3.shape
    tq, sub, heads = ATTN_TQ, ATTN_SUB, ATTN_HEADS_PER_STEP
    width = heads * HEAD_DIM
    groups = n_heads // heads
    r = jnp.arange(2 * sub) % sub
    c = jnp.arange(2 * sub)
    tri = -jnp.where(c[None, :] < sub, r[:, None] >= c[None, :], True).astype(BF16)
    kern = functools.partial(_attn_kernel, heads=heads, tq=tq, sub=sub)
    return pl.pallas_call(
        kern,
        grid=(b, groups, s // tq),
        in_specs=[
            pl.BlockSpec((1, tq, width), lambda bi, g, qi: (bi, qi, g)),
            pl.BlockSpec((1, s, width), lambda bi, g, qi: (bi, 0, groups + g)),
            pl.BlockSpec((1, s, width), lambda bi, g, qi: (bi, 0, 2 * groups + g)),
            pl.BlockSpec((2 * sub, 2 * sub), lambda bi, g, qi: (0, 0)),
        ],
        out_specs=pl.BlockSpec((1, tq, width), lambda bi, g, qi: (bi, qi, g)),
        out_shape=jax.ShapeDtypeStruct((b, s, n_heads * HEAD_DIM), BF16),
        scratch_shapes=[
            pltpu.VMEM((heads, tq, tq), F32), pltpu.VMEM((heads, tq, tq), F32),
            pltpu.VMEM((heads, (tq // sub) * tq, 2 * sub), BF16),
            pltpu.VMEM((heads, (tq // sub) * tq, 2 * sub), BF16),
            pltpu.VMEM((heads, tq, HEAD_DIM), F32), pltpu.VMEM((heads, tq, HEAD_DIM), F32),
        ],
        compiler_params=_params("parallel", "parallel", "arbitrary"),
        name="attention",
    )(qkv3, qkv3, qkv3, tri)


def _conv_kernel(u_ref, halo_ref, w_ref, bdw_ref, g_ref, bln_ref, o_ref, xs_ref, *, ts, rows):
    i = pl.program_id(1)
    xs_ref[0:CONV_HALO, :] = jnp.where(i > 0, halo_ref[0], 0.0)
    xs_ref[CONV_HALO:, :] = u_ref[0]
    first = CONV_HALO - (CONV_WIDTH - 1)
    for r0 in range(0, ts, rows):
        acc = xs_ref[pl.ds(first + r0, rows), :] * w_ref[0:1, :]
        for j in range(1, CONV_WIDTH):
            acc = acc + xs_ref[pl.ds(first + r0 + j, rows), :] * w_ref[j:j + 1, :]
        acc = acc + bdw_ref[...]
        mu = jnp.mean(acc, axis=-1, keepdims=True)
        cen = acc - mu
        var = jnp.mean(cen * cen, axis=-1, keepdims=True)
        y = cen * lax.rsqrt(var + EPS) * g_ref[...] + bln_ref[...]
        o_ref[0, pl.ds(r0, rows), :] = (y * jax.nn.sigmoid(y)).astype(o_ref.dtype)


def _conv_branch(u3, w_dw, b_dw, g_ln, b_ln):
    b, s, c = u3.shape
    ts, rows = CONV_TS, CONV_ROWS
    ratio = ts // CONV_HALO
    kern = functools.partial(_conv_kernel, ts=ts, rows=rows)
    vec = pl.BlockSpec((1, c), lambda bi, i: (0, 0))
    return pl.pallas_call(
        kern,
        grid=(b, s // ts),
        in_specs=[
            pl.BlockSpec((1, ts, c), lambda bi, i: (bi, i, 0)),
            pl.BlockSpec((1, CONV_HALO, c), lambda bi, i: (bi, jnp.maximum(i * ratio - 1, 0), 0)),
            pl.BlockSpec((CONV_WIDTH, c), lambda bi, i: (0, 0)),
            vec, vec, vec,
        ],
        out_specs=pl.BlockSpec((1, ts, c), lambda bi, i: (bi, i, 0)),
        out_shape=jax.ShapeDtypeStruct((b, s, c), BF16),
        scratch_shapes=[pltpu.VMEM((ts + CONV_HALO, c), F32)],
        compiler_params=_params("parallel", "arbitrary"),
        name="conv",
    )(u3, u3, w_dw, b_dw, g_ln, b_ln)


def _merge_kernel(x_ref, osb_ref, ocv_ref, gsb_ref, gcv_ref, wsb_ref, wcv_ref, wo_ref, g_ref, o_ref):
    a = jnp.dot(osb_ref[...], wsb_ref[...], preferred_element_type=F32)
    c = jnp.dot(ocv_ref[...], wcv_ref[...], preferred_element_type=F32)
    merged = gsb_ref[...].astype(F32) * a + gcv_ref[...].astype(F32) * c
    y = jnp.dot(merged.astype(BF16), wo_ref[...], preferred_element_type=F32)
    o_ref[...] = x_ref[...] + y * _rms_scale(y) * g_ref[...]


def _merge(x2, osb, ocv, gates, w_sb, w_cv, w_o, g):
    m, d = x2.shape
    dh = osb.shape[1]
    tm = MERGE_TM
    resident = lambda shape: pl.BlockSpec(shape, lambda i: (0, 0), pipeline_mode=pl.Buffered(1))
    return pl.pallas_call(
        _merge_kernel,
        grid=(m // tm,),
        in_specs=[
            pl.BlockSpec((tm, d), lambda i: (i, 0)),
            pl.BlockSpec((tm, dh), lambda i: (i, 0)),
            pl.BlockSpec((tm, dh), lambda i: (i, 0)),
            pl.BlockSpec((tm, d), lambda i: (i, 0)),
            pl.BlockSpec((tm, d), lambda i: (i, 1)),
            resident((dh, d)), resident((dh, d)), resident((d, d)), resident((1, d)),
        ],
        out_specs=pl.BlockSpec((tm, d), lambda i: (i, 0)),
        out_shape=jax.ShapeDtypeStruct((m, d), F32),
        compiler_params=_params("parallel"),
        name="merge",
    )(x2, osb, ocv, gates, gates, w_sb, w_cv, w_o, g)


def _mlp_kernel(x_ref, g1_ref, wu_ref, wd_ref, g2_ref, o_ref, h_ref, acc_ref):
    f = pl.program_id(1)

    @pl.when(f == 0)
    def _():
        x = x_ref[...]
        h_ref[...] = (x * _rms_scale(x) * g1_ref[...]).astype(BF16)

    t = jnp.dot(h_ref[...], wu_ref[...], preferred_element_type=F32)
    t = jnp.square(jnp.maximum(t, 0.0)).astype(BF16)
    d = jnp.dot(t, wd_ref[...], preferred_element_type=F32)

    @pl.when(f == 0)
    def _():
        acc_ref[...] = d

    @pl.when(f > 0)
    def _():
        acc_ref[...] += d

    @pl.when(f == pl.num_programs(1) - 1)
    def _():
        y = acc_ref[...]
        o_ref[...] = x_ref[...] + y * _rms_scale(y) * g2_ref[...]


def _mlp(x2, g1, w_up, w_down, g2):
    m, d = x2.shape
    dff = w_up.shape[1]
    tm, tf = MLP_TM, MLP_TF
    vec = pl.BlockSpec((1, d), lambda i, f: (0, 0))
    return pl.pallas_call(
        _mlp_kernel,
        grid=(m // tm, dff // tf),
        in_specs=[
            pl.BlockSpec((tm, d), lambda i, f: (i, 0)),
            vec,
            pl.BlockSpec((d, tf), lambda i, f: (0, f)),
            pl.BlockSpec((tf, d), lambda i, f: (f, 0)),
            vec,
        ],
        out_specs=pl.BlockSpec((tm, d), lambda i, f: (i, 0)),
        out_shape=jax.ShapeDtypeStruct((m, d), F32),
        scratch_shapes=[pltpu.VMEM((tm, d), BF16), pltpu.VMEM((tm, d), F32)],
        compiler_params=_params("parallel", "arbitrary"),
        name="mlp",
    )(x2, g1, w_up, w_down, g2)


def kernel(x, g_pre_mix, w_in, b_in, w_dw, b_dw, g_conv_ln, b_conv_ln, w_sb_out, w_conv_out, w_o,
           g_post_mix, g_pre_mlp, w_up, w_down, g_post_mlp):
    b, s, d = x.shape
    d_sb = w_sb_out.shape[1]
    d_conv = w_dw.shape[2]
    n_heads = d_sb // HEAD_DIM
    row = lambda v: v.reshape(1, -1)
    x2 = x.reshape(b * s, d)
    for l in range(w_in.shape[0]):
        qkv, u, gates = _inproj(x2, row(g_pre_mix[l]), w_in[l].astype(BF16), row(b_in[l]))
        o_sb = _attention(qkv.reshape(b, s, 3 * d_sb), n_heads)
        o_cv = _conv_branch(u.reshape(b, s, d_conv), w_dw[l], row(b_dw[l]), row(g_conv_ln[l]),
                            row(b_conv_ln[l]))
        x2 = _merge(x2, o_sb.reshape(b * s, d_sb), o_cv.reshape(b * s, d_conv), gates,
                    w_sb_out[l].astype(BF16), w_conv_out[l].astype(BF16), w_o[l].astype(BF16),
                    row(g_post_mix[l]))
        x2 = _mlp(x2, row(g_pre_mlp[l]), w_up[l].astype(BF16), w_down[l].astype(BF16),
                  row(g_post_mlp[l]))
    return x2.reshape(b, s, d)
```

```python
import functools
import math

import jax
import jax.numpy as jnp
from jax import lax
from jax.experimental import pallas as pl
from jax.experimental.pallas import tpu as pltpu

F32 = jnp.float32
BF16 = jnp.bfloat16

EPS = 1e-6
LOG2E = 1.4426950408889634
HEAD_DIM = 128
SUBLANES = 8
CONV_WIDTH = 31
CONV_HALO = 32

VMEM_LIMIT_BYTES = 56 * 1024 * 1024

INPROJ_TM = 1024
INPROJ_TN = 1024
ATTN_TQ = 256
ATTN_SUB = 128
ATTN_HEADS_PER_STEP = 4
CONV_TS = 256
CONV_ROWS = 32
MERGE_TM = 512
MLP_TM = 512
MLP_TF = 1024


def _params(*sem):
    return pltpu.CompilerParams(dimension_semantics=sem, vmem_limit_bytes=VMEM_LIMIT_BYTES)


def _rms_scale(x):
    return lax.rsqrt(jnp.mean(x * x, axis=-1, keepdims=True) + EPS)


def _inproj_kernel(x_ref, g_ref, w_ref, b_ref, qkv_ref, u_ref, gates_ref, h_ref, a_ref):
    j = pl.program_id(1)

    @pl.when(j == 0)
    def _():
        x = x_ref[...]
        h_ref[...] = (x * _rms_scale(x) * g_ref[...]).astype(BF16)

    def proj():
        return jnp.dot(h_ref[...], w_ref[...], preferred_element_type=F32) + b_ref[...]

    @pl.when(j < 3)
    def _():
        qkv_ref[...] = proj().astype(BF16)

    @pl.when(j == 3)
    def _():
        a_ref[...] = proj()

    @pl.when(j == 4)
    def _():
        u_ref[...] = a_ref[...] * jax.nn.sigmoid(proj())

    @pl.when(j >= 5)
    def _():
        gates_ref[...] = jax.nn.sigmoid(proj()).astype(BF16)


def _inproj(x2, g, w, b):
    m, d = x2.shape
    tm, tn = INPROJ_TM, INPROJ_TN
    n_tiles = w.shape[1] // tn
    assert n_tiles == 9 and tn == 1024
    return pl.pallas_call(
        _inproj_kernel,
        grid=(m // tm, n_tiles),
        in_specs=[
            pl.BlockSpec((tm, d), lambda i, j: (i, 0)),
            pl.BlockSpec((1, d), lambda i, j: (0, 0)),
            pl.BlockSpec((d, tn), lambda i, j: (0, j)),
            pl.BlockSpec((1, tn), lambda i, j: (0, j)),
        ],
        out_specs=[
            pl.BlockSpec((tm, tn), lambda i, j: (i, jnp.minimum(j, 2))),
            pl.BlockSpec((tm, tn), lambda i, j: (i, 0)),
            pl.BlockSpec((tm, tn), lambda i, j: (i, jnp.clip(j - 5, 0, 3))),
        ],
        out_shape=[
            jax.ShapeDtypeStruct((m, 3 * tn), BF16),
            jax.ShapeDtypeStruct((m, tn), F32),
            jax.ShapeDtypeStruct((m, 4 * tn), BF16),
        ],
        scratch_shapes=[pltpu.VMEM((tm, d), BF16), pltpu.VMEM((tm, tn), F32)],
        compiler_params=_params("parallel", "arbitrary"),
        name="inproj",
    )(x2, g, w, b)


def _attn_kernel(q_ref, k_ref, v_ref, tri_ref, o_ref, z_a, z_b, l_a, l_b, carry_ref, acc_ref,
                 *, heads, tq, sub):
    qi = pl.program_id(2)
    scale = 1.0 / math.sqrt(HEAD_DIM)
    nsub = tq // sub
    row = lax.broadcasted_iota(jnp.int32, (tq, tq), 0)
    col = lax.broadcasted_iota(jnp.int32, (tq, tq), 1)
    causal = col < row

    def key_start(m):
        return pl.multiple_of((qi - 1 - m) * tq, tq)

    def scores(h, k0, z_buf, l_buf, diag=False):
        lanes = pl.ds(h * HEAD_DIM, HEAD_DIM)
        kb = k_ref[0, pl.ds(k0, tq), lanes]
        z = lax.dot_general(q_ref[0, :, lanes], kb, (((1,), (1,)), ((), ())),
                            preferred_element_type=F32) * scale
        z_buf[h] = z
        softplus = jnp.maximum(z, 0.0) + jnp.log(1.0 + jnp.exp2(jnp.abs(z) * (-LOG2E)))
        if diag:
            softplus = jnp.where(causal, softplus, 0.0)
        hi = softplus.astype(BF16)
        lo = (softplus - hi.astype(F32)).astype(BF16)
        for s in range(nsub):
            l_buf[h, s * tq:(s + 1) * tq, 0:sub] = hi[:, s * sub:(s + 1) * sub]
            l_buf[h, s * tq:(s + 1) * tq, sub:2 * sub] = lo[:, s * sub:(s + 1) * sub]

    def suffix_sums(h, l_buf):
        return jnp.dot(l_buf[h], tri_ref[...], preferred_element_type=F32)

    def weights_times_v(h, k0, z_buf, cs, diag=False):
        carry = jnp.zeros((tq, HEAD_DIM), F32) if diag else carry_ref[h]
        parts = [None] * nsub
        for s in reversed(range(nsub)):
            blk = cs[s * tq:(s + 1) * tq]
            parts[s] = jnp.exp(z_buf[h, :, s * sub:(s + 1) * sub] + blk[:, :sub] + carry)
            carry = carry + blk[:, sub:]
        carry_ref[h] = carry
        a = jnp.concatenate(parts, axis=1)
        if diag:
            a = jnp.where(causal, a, 0.0)
        vb = v_ref[0, pl.ds(k0, tq), pl.ds(h * HEAD_DIM, HEAD_DIM)]
        av = jnp.dot(a.astype(BF16), vb, preferred_element_type=F32)
        if diag:
            acc_ref[h] = av
        else:
            acc_ref[h] += av

    def step(nxt=None, cur=None, diag=False):
        if nxt is not None and cur is not None:
            for h in range(heads):
                scores(h, *nxt, diag=diag)
                weights_times_v(h, cur[0], cur[1], suffix_sums(h, cur[2]), diag=diag)
        elif nxt is not None:
            for h in range(heads):
                scores(h, *nxt, diag=diag)
        else:
            css = [suffix_sums(h, cur[2]) for h in range(heads)]
            for h in range(heads):
                weights_times_v(h, cur[0], cur[1], css[h], diag=diag)

    k_diag = pl.multiple_of(qi * tq, tq)
    buf_a, buf_b = (z_a, l_a), (z_b, l_b)
    step(nxt=(k_diag,) + buf_a, diag=True)
    step(cur=(k_diag,) + buf_a, diag=True)

    @pl.when(qi > 0)
    def _():
        step(nxt=(key_start(0),) + buf_a)

        def pair(p, _):
            m = 2 * p
            step(nxt=(key_start(m + 1),) + buf_b, cur=(key_start(m),) + buf_a)
            step(nxt=(key_start(m + 2),) + buf_a, cur=(key_start(m + 1),) + buf_b)
            return 0

        n_pairs = (qi - 1) // 2
        lax.fori_loop(0, n_pairs, pair, 0)
        m = 2 * n_pairs

        @pl.when(qi - m == 1)
        def _():
            step(cur=(key_start(m),) + buf_a)

        @pl.when(qi - m == 2)
        def _():
            step(nxt=(key_start(m + 1),) + buf_b, cur=(key_start(m),) + buf_a)
            step(cur=(key_start(m + 1),) + buf_b)

    for h in range(heads):
        o_ref[0, :, pl.ds(h * HEAD_DIM, HEAD_DIM)] = acc_ref[h].astype(o_ref.dtype)


def _attention(qkv3, n_heads):
    b, s, _ = qkv3.shape
    tq, sub, heads = ATTN_TQ, ATTN_SUB, ATTN_HEADS_PER_STEP
    width = heads * HEAD_DIM
    groups = n_heads // heads
    r = jnp.arange(2 * sub) % sub
    c = jnp.arange(2 * sub)
    tri = -jnp.where(c[None, :] < sub, r[:, None] >= c[None, :], True).astype(BF16)
    kern = functools.partial(_attn_kernel, heads=heads, tq=tq, sub=sub)
    return pl.pallas_call(
        kern,
        grid=(b, groups, s // tq),
        in_specs=[
            pl.BlockSpec((1, tq, width), lambda bi, g, qi: (bi, qi, g)),
            pl.BlockSpec((1, s, width), lambda bi, g, qi: (bi, 0, groups + g)),
            pl.BlockSpec((1, s, width), lambda bi, g, qi: (bi, 0, 2 * groups + g)),
            pl.BlockSpec((2 * sub, 2 * sub), lambda bi, g, qi: (0, 0)),
        ],
        out_specs=pl.BlockSpec((1, tq, width), lambda bi, g, qi: (bi, qi, g)),
        out_shape=jax.ShapeDtypeStruct((b, s, n_heads * HEAD_DIM), BF16),
        scratch_shapes=[
            pltpu.VMEM((heads, tq, tq), F32), pltpu.VMEM((heads, tq, tq), F32),
            pltpu.VMEM((heads, (tq // sub) * tq, 2 * sub), BF16),
            pltpu.VMEM((heads, (tq // sub) * tq, 2 * sub), BF16),
            pltpu.VMEM((heads, tq, HEAD_DIM), F32), pltpu.VMEM((heads, tq, HEAD_DIM), F32),
        ],
        compiler_params=_params("parallel", "parallel", "arbitrary"),
        name="attention",
    )(qkv3, qkv3, qkv3, tri)


def _conv_kernel(*refs, ts, rows, n_cast):
    u_ref, halo_ref, w_ref, bdw_ref, g_ref, bln_ref = refs[:6]
    cast_in = refs[6:6 + n_cast]
    o_ref = refs[6 + n_cast]
    cast_out = refs[7 + n_cast:7 + 2 * n_cast]
    xs_ref, wb_ref = refs[7 + 2 * n_cast:]

    for src, dst in zip(cast_in, cast_out):
        dst[...] = src[...].astype(dst.dtype)

    i = pl.program_id(1)
    n = ts + CONV_HALO
    xs_ref[0, 0:CONV_HALO, :] = jnp.where(i > 0, halo_ref[0], 0.0)
    xs_ref[0, CONV_HALO:, :] = u_ref[0]
    padded = xs_ref[0]
    for p in range(1, SUBLANES):
        xs_ref[p] = pltpu.roll(padded, n - p, axis=0)
    c = w_ref.shape[1]
    for j in range(CONV_WIDTH):
        wb_ref[j] = jnp.broadcast_to(w_ref[j:j + 1, :], (SUBLANES, c))
    first = CONV_HALO - (CONV_WIDTH - 1)
    for r0 in range(0, ts, rows):
        acc = None
        for j in range(CONV_WIDTH):
            p, base = (first + j) % SUBLANES, (first + j) // SUBLANES * SUBLANES
            window = xs_ref[p, pl.ds(base + r0, rows), :].reshape(rows // SUBLANES, SUBLANES, c)
            term = window * wb_ref[j]
            acc = term if acc is None else acc + term
        acc = acc.reshape(rows, c) + bdw_ref[...]
        mu = jnp.mean(acc, axis=-1, keepdims=True)
        cen = acc - mu
        var = jnp.mean(cen * cen, axis=-1, keepdims=True)
        y = cen * lax.rsqrt(var + EPS) * g_ref[...] + bln_ref[...]
        o_ref[0, pl.ds(r0, rows), :] = (y * jax.nn.sigmoid(y)).astype(o_ref.dtype)


def _conv_branch(u3, w_dw, b_dw, g_ln, b_ln, weights_f32):
    b, s, c = u3.shape
    ts, rows = CONV_TS, CONV_ROWS
    n_i = s // ts
    steps = b * n_i
    ratio = ts // CONV_HALO
    kern = functools.partial(_conv_kernel, ts=ts, rows=rows, n_cast=len(weights_f32))
    vec = pl.BlockSpec((1, c), lambda bi, i: (0, 0))
    slab = lambda w: pl.BlockSpec((w.shape[0] // steps, w.shape[1]), lambda bi, i: (bi * n_i + i, 0))
    for w in weights_f32:
        assert w.shape[0] % (steps * 16) == 0, w.shape
    outs = pl.pallas_call(
        kern,
        grid=(b, n_i),
        in_specs=[
            pl.BlockSpec((1, ts, c), lambda bi, i: (bi, i, 0)),
            pl.BlockSpec((1, CONV_HALO, c), lambda bi, i: (bi, jnp.maximum(i * ratio - 1, 0), 0)),
            pl.BlockSpec((CONV_WIDTH, c), lambda bi, i: (0, 0)),
            vec, vec, vec,
        ] + [slab(w) for w in weights_f32],
        out_specs=[pl.BlockSpec((1, ts, c), lambda bi, i: (bi, i, 0))] + [slab(w) for w in weights_f32],
        out_shape=[jax.ShapeDtypeStruct((b, s, c), BF16)]
        + [jax.ShapeDtypeStruct(w.shape, BF16) for w in weights_f32],
        scratch_shapes=[pltpu.VMEM((SUBLANES, ts + CONV_HALO, c), F32),
                        pltpu.VMEM((CONV_WIDTH, SUBLANES, c), F32)],
        compiler_params=_params("parallel", "arbitrary"),
        name="conv",
    )(u3, u3, w_dw, b_dw, g_ln, b_ln, *weights_f32)
    return outs[0], outs[1:]


def _merge_kernel(x_ref, osb_ref, ocv_ref, gsb_ref, gcv_ref, wsb_ref, wcv_ref, wo_ref, g_ref, o_ref):
    a = jnp.dot(osb_ref[...], wsb_ref[...], preferred_element_type=F32)
    c = jnp.dot(ocv_ref[...], wcv_ref[...], preferred_element_type=F32)
    merged = gsb_ref[...].astype(F32) * a + gcv_ref[...].astype(F32) * c
    y = jnp.dot(merged.astype(BF16), wo_ref[...], preferred_element_type=F32)
    o_ref[...] = x_ref[...] + y * _rms_scale(y) * g_ref[...]


def _merge(x2, osb, ocv, gates, w_sb, w_cv, w_o, g):
    m, d = x2.shape
    dh = osb.shape[1]
    tm = MERGE_TM
    resident = lambda shape: pl.BlockSpec(shape, lambda i: (0, 0), pipeline_mode=pl.Buffered(1))
    return pl.pallas_call(
        _merge_kernel,
        grid=(m // tm,),
        in_specs=[
            pl.BlockSpec((tm, d), lambda i: (i, 0)),
            pl.BlockSpec((tm, dh), lambda i: (i, 0)),
            pl.BlockSpec((tm, dh), lambda i: (i, 0)),
            pl.BlockSpec((tm, d), lambda i: (i, 0)),
            pl.BlockSpec((tm, d), lambda i: (i, 1)),
            resident((dh, d)), resident((dh, d)), resident((d, d)), resident((1, d)),
        ],
        out_specs=pl.BlockSpec((tm, d), lambda i: (i, 0)),
        out_shape=jax.ShapeDtypeStruct((m, d), F32),
        compiler_params=_params("parallel"),
        name="merge",
    )(x2, osb, ocv, gates, gates, w_sb, w_cv, w_o, g)


def _mlp_kernel(x_ref, g1_ref, wu_ref, wd_ref, g2_ref, o_ref, h_ref):
    f = pl.program_id(1)

    @pl.when(f == 0)
    def _():
        x = x_ref[...]
        h_ref[...] = (x * _rms_scale(x) * g1_ref[...]).astype(BF16)
        o_ref[...] = jnp.zeros_like(o_ref)

    t = jnp.dot(h_ref[...], wu_ref[...], preferred_element_type=F32)
    t = jnp.square(jnp.maximum(t, 0.0)).astype(BF16)
    o_ref[...] += jnp.dot(t, wd_ref[...], preferred_element_type=F32)

    @pl.when(f == pl.num_programs(1) - 1)
    def _():
        y = o_ref[...]
        o_ref[...] = x_ref[...] + y * _rms_scale(y) * g2_ref[...]


def _mlp(x2, g1, w_up, w_down, g2):
    m, d = x2.shape
    dff = w_up.shape[1]
    tm, tf = MLP_TM, MLP_TF
    vec = pl.BlockSpec((1, d), lambda i, f: (0, 0))
    return pl.pallas_call(
        _mlp_kernel,
        grid=(m // tm, dff // tf),
        in_specs=[
            pl.BlockSpec((tm, d), lambda i, f: (i, 0)),
            vec,
            pl.BlockSpec((d, tf), lambda i, f: (0, f)),
            pl.BlockSpec((tf, d), lambda i, f: (f, 0)),
            vec,
        ],
        out_specs=pl.BlockSpec((tm, d), lambda i, f: (i, 0)),
        out_shape=jax.ShapeDtypeStruct((m, d), F32),
        scratch_shapes=[pltpu.VMEM((tm, d), BF16)],
        compiler_params=_params("parallel", "arbitrary"),
        name="mlp",
    )(x2, g1, w_up, w_down, g2)


def kernel(x, g_pre_mix, w_in, b_in, w_dw, b_dw, g_conv_ln, b_conv_ln, w_sb_out, w_conv_out, w_o,
           g_post_mix, g_pre_mlp, w_up, w_down, g_post_mlp):
    b, s, d = x.shape
    d_sb = w_sb_out.shape[1]
    d_conv = w_dw.shape[2]
    n_heads = d_sb // HEAD_DIM
    row = lambda v: v.reshape(1, -1)
    x2 = x.reshape(b * s, d)
    for l in range(w_in.shape[0]):
        qkv, u, gates = _inproj(x2, row(g_pre_mix[l]), w_in[l].astype(BF16), row(b_in[l]))
        o_sb = _attention(qkv.reshape(b, s, 3 * d_sb), n_heads)
        o_cv, (w_sb, w_cv, w_out, w_u, w_d) = _conv_branch(
            u.reshape(b, s, d_conv), w_dw[l], row(b_dw[l]), row(g_conv_ln[l]), row(b_conv_ln[l]),
            (w_sb_out[l], w_conv_out[l], w_o[l], w_up[l], w_down[l]))
        x2 = _merge(x2, o_sb.reshape(b * s, d_sb), o_cv.reshape(b * s, d_conv), gates,
                    w_sb, w_cv, w_out, row(g_post_mix[l]))
        x2 = _mlp(x2, row(g_pre_mlp[l]), w_u, w_d, row(g_post_mlp[l]))
    return x2.reshape(b, s, d)
```

```python
import functools
import math

import jax
import jax.numpy as jnp
from jax import lax
from jax.experimental import pallas as pl
from jax.experimental.pallas import tpu as pltpu

F32 = jnp.float32
BF16 = jnp.bfloat16

EPS = 1e-6
LOG2E = 1.4426950408889634
HEAD_DIM = 128
SUBLANES = 8
CONV_WIDTH = 31
CONV_HALO = 32

VMEM_LIMIT_BYTES = 56 * 1024 * 1024

INPROJ_TM = 1024
INPROJ_TN = 1024
ATTN_TQ = 256
ATTN_SUB = 128
ATTN_HEADS_PER_STEP = 8
CONV_TS = 256
CONV_ROWS = 32
MERGE_TM = 512
MLP_TM = 512
MLP_TF = 1024


def _params(*sem):
    return pltpu.CompilerParams(dimension_semantics=sem, vmem_limit_bytes=VMEM_LIMIT_BYTES)


def _rms_scale(x):
    return lax.rsqrt(jnp.mean(x * x, axis=-1, keepdims=True) + EPS)


def _inproj_kernel(x_ref, g_ref, w_ref, b_ref, qkv_ref, u_ref, gates_ref, h_ref, a_ref):
    j = pl.program_id(1)

    @pl.when(j == 0)
    def _():
        x = x_ref[...]
        h_ref[...] = (x * _rms_scale(x) * g_ref[...]).astype(BF16)

    def proj():
        return jnp.dot(h_ref[...], w_ref[...], preferred_element_type=F32) + b_ref[...]

    @pl.when(j < 3)
    def _():
        qkv_ref[...] = proj().astype(BF16)

    @pl.when(j == 3)
    def _():
        a_ref[...] = proj()

    @pl.when(j == 4)
    def _():
        u_ref[...] = a_ref[...] * jax.nn.sigmoid(proj())

    @pl.when(j >= 5)
    def _():
        gates_ref[...] = jax.nn.sigmoid(proj()).astype(BF16)


def _inproj(x2, g, w, b):
    m, d = x2.shape
    tm, tn = INPROJ_TM, INPROJ_TN
    n_tiles = w.shape[1] // tn
    assert n_tiles == 9 and tn == 1024
    return pl.pallas_call(
        _inproj_kernel,
        grid=(m // tm, n_tiles),
        in_specs=[
            pl.BlockSpec((tm, d), lambda i, j: (i, 0)),
            pl.BlockSpec((1, d), lambda i, j: (0, 0)),
            pl.BlockSpec((d, tn), lambda i, j: (0, j)),
            pl.BlockSpec((1, tn), lambda i, j: (0, j)),
        ],
        out_specs=[
            pl.BlockSpec((tm, tn), lambda i, j: (i, jnp.minimum(j, 2))),
            pl.BlockSpec((tm, tn), lambda i, j: (i, 0)),
            pl.BlockSpec((tm, tn), lambda i, j: (i, jnp.clip(j - 5, 0, 3))),
        ],
        out_shape=[
            jax.ShapeDtypeStruct((m, 3 * tn), BF16),
            jax.ShapeDtypeStruct((m, tn), F32),
            jax.ShapeDtypeStruct((m, 4 * tn), BF16),
        ],
        scratch_shapes=[pltpu.VMEM((tm, d), BF16), pltpu.VMEM((tm, tn), F32)],
        compiler_params=_params("parallel", "arbitrary"),
        name="inproj",
    )(x2, g, w, b)


def _attn_kernel(q_ref, k_ref, v_ref, tri_ref, o_ref, z_a, z_b, l_a, l_b, carry_ref, acc_ref,
                 *, heads, tq, sub):
    qi = pl.program_id(2)
    scale = 1.0 / math.sqrt(HEAD_DIM)
    nsub = tq // sub
    row = lax.broadcasted_iota(jnp.int32, (tq, tq), 0)
    col = lax.broadcasted_iota(jnp.int32, (tq, tq), 1)
    causal = col < row

    def key_start(n):
        return pl.multiple_of((qi - n) * tq, tq)

    def scores(h, k0, z_buf, l_buf, diag=False):
        lanes = pl.ds(h * HEAD_DIM, HEAD_DIM)
        kb = k_ref[0, pl.ds(k0, tq), lanes]
        z = lax.dot_general(q_ref[0, :, lanes], kb, (((1,), (1,)), ((), ())),
                            preferred_element_type=F32) * scale
        z_buf[h] = z
        softplus = jnp.maximum(z, 0.0) + jnp.log(1.0 + jnp.exp2(jnp.abs(z) * (-LOG2E)))
        if diag:
            softplus = jnp.where(causal, softplus, 0.0)
        hi = softplus.astype(BF16)
        lo = (softplus - hi.astype(F32)).astype(BF16)
        for s in range(nsub):
            l_buf[h, s * tq:(s + 1) * tq, 0:sub] = hi[:, s * sub:(s + 1) * sub]
            l_buf[h, s * tq:(s + 1) * tq, sub:2 * sub] = lo[:, s * sub:(s + 1) * sub]

    def suffix_sums(h, l_buf):
        return jnp.dot(l_buf[h], tri_ref[...], preferred_element_type=F32)

    def weights_times_v(h, k0, z_buf, cs, diag=False):
        carry = jnp.zeros((tq, HEAD_DIM), F32) if diag else carry_ref[h]
        parts = [None] * nsub
        for s in reversed(range(nsub)):
            blk = cs[s * tq:(s + 1) * tq]
            parts[s] = jnp.exp(z_buf[h, :, s * sub:(s + 1) * sub] + blk[:, :sub] + carry)
            carry = carry + blk[:, sub:]
        carry_ref[h] = carry
        a = jnp.concatenate(parts, axis=1)
        if diag:
            a = jnp.where(causal, a, 0.0)
        vb = v_ref[0, pl.ds(k0, tq), pl.ds(h * HEAD_DIM, HEAD_DIM)]
        av = jnp.dot(a.astype(BF16), vb, preferred_element_type=F32)
        if diag:
            acc_ref[h] = av
        else:
            acc_ref[h] += av

    def step(nxt=None, cur=None, cur_diag=False, nxt_diag=False):
        if nxt is not None and cur is not None:
            for h in range(heads):
                scores(h, *nxt, diag=nxt_diag)
                weights_times_v(h, cur[0], cur[1], suffix_sums(h, cur[2]), diag=cur_diag)
        elif nxt is not None:
            for h in range(heads):
                scores(h, *nxt, diag=nxt_diag)
        else:
            css = [suffix_sums(h, cur[2]) for h in range(heads)]
            for h in range(heads):
                weights_times_v(h, cur[0], cur[1], css[h], diag=cur_diag)

    buf = ((z_a, l_a), (z_b, l_b))
    tile = lambda n, parity: (key_start(n),) + buf[parity]
    step(nxt=tile(0, 0), nxt_diag=True)

    @pl.when(qi == 0)
    def _():
        step(cur=tile(0, 0), cur_diag=True)

    @pl.when(qi > 0)
    def _():
        step(nxt=tile(1, 1), cur=tile(0, 0), cur_diag=True)

        def pair(p, _):
            n = 2 * p + 1
            step(nxt=tile(n + 1, 0), cur=tile(n, 1))
            step(nxt=tile(n + 2, 1), cur=tile(n + 1, 0))
            return 0

        n_pairs = (qi - 1) // 2
        lax.fori_loop(0, n_pairs, pair, 0)
        n = 2 * n_pairs + 1

        @pl.when(qi - n == 0)
        def _():
            step(cur=tile(n, 1))

        @pl.when(qi - n == 1)
        def _():
            step(nxt=tile(n + 1, 0), cur=tile(n, 1))
            step(cur=tile(n + 1, 0))

    for h in range(heads):
        o_ref[0, :, pl.ds(h * HEAD_DIM, HEAD_DIM)] = acc_ref[h].astype(o_ref.dtype)


def _attention(qkv3, n_heads):
    b, s, _ = qkv3.shape
    tq, sub, heads = ATTN_TQ, ATTN_SUB, ATTN_HEADS_PER_STEP
    width = heads * HEAD_DIM
    groups = n_heads // heads
    r = jnp.arange(2 * sub) % sub
    c = jnp.arange(2 * sub)
    tri = -jnp.where(c[None, :] < sub, r[:, None] >= c[None, :], True).astype(BF16)
    kern = functools.partial(_attn_kernel, heads=heads, tq=tq, sub=sub)
    return pl.pallas_call(
        kern,
        grid=(b, groups, s // tq),
        in_specs=[
            pl.BlockSpec((1, tq, width), lambda bi, g, qi: (bi, qi, g)),
            pl.BlockSpec((1, s, width), lambda bi, g, qi: (bi, 0, groups + g)),
            pl.BlockSpec((1, s, width), lambda bi, g, qi: (bi, 0, 2 * groups + g)),
            pl.BlockSpec((2 * sub, 2 * sub), lambda bi, g, qi: (0, 0)),
        ],
        out_specs=pl.BlockSpec((1, tq, width), lambda bi, g, qi: (bi, qi, g)),
        out_shape=jax.ShapeDtypeStruct((b, s, n_heads * HEAD_DIM), BF16),
        scratch_shapes=[
            pltpu.VMEM((heads, tq, tq), F32), pltpu.VMEM((heads, tq, tq), F32),
            pltpu.VMEM((heads, (tq // sub) * tq, 2 * sub), BF16),
            pltpu.VMEM((heads, (tq // sub) * tq, 2 * sub), BF16),
            pltpu.VMEM((heads, tq, HEAD_DIM), F32), pltpu.VMEM((heads, tq, HEAD_DIM), F32),
        ],
        compiler_params=_params("parallel", "parallel", "arbitrary"),
        name="attention",
    )(qkv3, qkv3, qkv3, tri)


def _conv_kernel(*refs, ts, rows, n_cast):
    u_ref, halo_ref, w_ref, bdw_ref, g_ref, bln_ref = refs[:6]
    cast_in = refs[6:6 + n_cast]
    o_ref = refs[6 + n_cast]
    cast_out = refs[7 + n_cast:7 + 2 * n_cast]
    xs_ref, wb_ref = refs[7 + 2 * n_cast:]

    for src, dst in zip(cast_in, cast_out):
        dst[...] = src[...].astype(dst.dtype)

    i = pl.program_id(1)
    n = ts + CONV_HALO
    xs_ref[0, 0:CONV_HALO, :] = jnp.where(i > 0, halo_ref[0], 0.0)
    xs_ref[0, CONV_HALO:, :] = u_ref[0]
    padded = xs_ref[0]
    for p in range(1, SUBLANES):
        xs_ref[p] = pltpu.roll(padded, n - p, axis=0)
    c = w_ref.shape[1]
    for j in range(CONV_WIDTH):
        wb_ref[j] = jnp.broadcast_to(w_ref[j:j + 1, :], (SUBLANES, c))
    first = CONV_HALO - (CONV_WIDTH - 1)
    for r0 in range(0, ts, rows):
        acc = None
        for j in range(CONV_WIDTH):
            p, base = (first + j) % SUBLANES, (first + j) // SUBLANES * SUBLANES
            window = xs_ref[p, pl.ds(base + r0, rows), :].reshape(rows // SUBLANES, SUBLANES, c)
            term = window * wb_ref[j]
            acc = term if acc is None else acc + term
        acc = acc.reshape(rows, c) + bdw_ref[...]
        mu = jnp.mean(acc, axis=-1, keepdims=True)
        cen = acc - mu
        var = jnp.mean(cen * cen, axis=-1, keepdims=True)
        y = cen * lax.rsqrt(var + EPS) * g_ref[...] + bln_ref[...]
        o_ref[0, pl.ds(r0, rows), :] = (y * jax.nn.sigmoid(y)).astype(o_ref.dtype)


def _conv_branch(u3, w_dw, b_dw, g_ln, b_ln, weights_f32):
    b, s, c = u3.shape
    ts, rows = CONV_TS, CONV_ROWS
    n_i = s // ts
    steps = b * n_i
    ratio = ts // CONV_HALO
    kern = functools.partial(_conv_kernel, ts=ts, rows=rows, n_cast=len(weights_f32))
    vec = pl.BlockSpec((1, c), lambda bi, i: (0, 0))
    slab = lambda w: pl.BlockSpec((w.shape[0] // steps, w.shape[1]), lambda bi, i: (bi * n_i + i, 0))
    for w in weights_f32:
        assert w.shape[0] % (steps * 16) == 0, w.shape
    outs = pl.pallas_call(
        kern,
        grid=(b, n_i),
        in_specs=[
            pl.BlockSpec((1, ts, c), lambda bi, i: (bi, i, 0)),
            pl.BlockSpec((1, CONV_HALO, c), lambda bi, i: (bi, jnp.maximum(i * ratio - 1, 0), 0)),
            pl.BlockSpec((CONV_WIDTH, c), lambda bi, i: (0, 0)),
            vec, vec, vec,
        ] + [slab(w) for w in weights_f32],
        out_specs=[pl.BlockSpec((1, ts, c), lambda bi, i: (bi, i, 0))] + [slab(w) for w in weights_f32],
        out_shape=[jax.ShapeDtypeStruct((b, s, c), BF16)]
        + [jax.ShapeDtypeStruct(w.shape, BF16) for w in weights_f32],
        scratch_shapes=[pltpu.VMEM((SUBLANES, ts + CONV_HALO, c), F32),
                        pltpu.VMEM((CONV_WIDTH, SUBLANES, c), F32)],
        compiler_params=_params("parallel", "arbitrary"),
        name="conv",
    )(u3, u3, w_dw, b_dw, g_ln, b_ln, *weights_f32)
    return outs[0], outs[1:]


def _merge_kernel(x_ref, osb_ref, ocv_ref, gsb_ref, gcv_ref, wsb_ref, wcv_ref, wo_ref, g_ref, o_ref):
    a = jnp.dot(osb_ref[...], wsb_ref[...], preferred_element_type=F32)
    c = jnp.dot(ocv_ref[...], wcv_ref[...], preferred_element_type=F32)
    merged = gsb_ref[...].astype(F32) * a + gcv_ref[...].astype(F32) * c
    y = jnp.dot(merged.astype(BF16), wo_ref[...], preferred_element_type=F32)
    o_ref[...] = x_ref[...] + y * _rms_scale(y) * g_ref[...]


def _merge(x2, osb, ocv, gates, w_sb, w_cv, w_o, g):
    m, d = x2.shape
    dh = osb.shape[1]
    tm = MERGE_TM
    resident = lambda shape: pl.BlockSpec(shape, lambda i: (0, 0), pipeline_mode=pl.Buffered(1))
    return pl.pallas_call(
        _merge_kernel,
        grid=(m // tm,),
        in_specs=[
            pl.BlockSpec((tm, d), lambda i: (i, 0)),
            pl.BlockSpec((tm, dh), lambda i: (i, 0)),
            pl.BlockSpec((tm, dh), lambda i: (i, 0)),
            pl.BlockSpec((tm, d), lambda i: (i, 0)),
            pl.BlockSpec((tm, d), lambda i: (i, 1)),
            resident((dh, d)), resident((dh, d)), resident((d, d)), resident((1, d)),
        ],
        out_specs=pl.BlockSpec((tm, d), lambda i: (i, 0)),
        out_shape=jax.ShapeDtypeStruct((m, d), F32),
        compiler_params=_params("parallel"),
        name="merge",
    )(x2, osb, ocv, gates, gates, w_sb, w_cv, w_o, g)


def _mlp_kernel(x_ref, g1_ref, wu_ref, wd_ref, g2_ref, o_ref, h_ref):
    f = pl.program_id(1)

    @pl.when(f == 0)
    def _():
        x = x_ref[...]
        h_ref[...] = (x * _rms_scale(x) * g1_ref[...]).astype(BF16)
        o_ref[...] = jnp.zeros_like(o_ref)

    t = jnp.dot(h_ref[...], wu_ref[...], preferred_element_type=F32)
    t = jnp.square(jnp.maximum(t, 0.0)).astype(BF16)
    o_ref[...] += jnp.dot(t, wd_ref[...], preferred_element_type=F32)

    @pl.when(f == pl.num_programs(1) - 1)
    def _():
        y = o_ref[...]
        o_ref[...] = x_ref[...] + y * _rms_scale(y) * g2_ref[...]


def _mlp(x2, g1, w_up, w_down, g2):
    m, d = x2.shape
    dff = w_up.shape[1]
    tm, tf = MLP_TM, MLP_TF
    vec = pl.BlockSpec((1, d), lambda i, f: (0, 0))
    return pl.pallas_call(
        _mlp_kernel,
        grid=(m // tm, dff // tf),
        in_specs=[
            pl.BlockSpec((tm, d), lambda i, f: (i, 0)),
            vec,
            pl.BlockSpec((d, tf), lambda i, f: (0, f)),
            pl.BlockSpec((tf, d), lambda i, f: (f, 0)),
            vec,
        ],
        out_specs=pl.BlockSpec((tm, d), lambda i, f: (i, 0)),
        out_shape=jax.ShapeDtypeStruct((m, d), F32),
        scratch_shapes=[pltpu.VMEM((tm, d), BF16)],
        compiler_params=_params("parallel", "arbitrary"),
        name="mlp",
    )(x2, g1, w_up, w_down, g2)


def kernel(x, g_pre_mix, w_in, b_in, w_dw, b_dw, g_conv_ln, b_conv_ln, w_sb_out, w_conv_out, w_o,
           g_post_mix, g_pre_mlp, w_up, w_down, g_post_mlp):
    b, s, d = x.shape
    d_sb = w_sb_out.shape[1]
    d_conv = w_dw.shape[2]
    n_heads = d_sb // HEAD_DIM
    row = lambda v: v.reshape(1, -1)
    x2 = x.reshape(b * s, d)
    for l in range(w_in.shape[0]):
        qkv, u, gates = _inproj(x2, row(g_pre_mix[l]), w_in[l].astype(BF16), row(b_in[l]))
        o_sb = _attention(qkv.reshape(b, s, 3 * d_sb), n_heads)
        o_cv, (w_sb, w_cv, w_out, w_u, w_d) = _conv_branch(
            u.reshape(b, s, d_conv), w_dw[l], row(b_dw[l]), row(g_conv_ln[l]), row(b_conv_ln[l]),
            (w_sb_out[l], w_conv_out[l], w_o[l], w_up[l], w_down[l]))
        x2 = _merge(x2, o_sb.reshape(b * s, d_sb), o_cv.reshape(b * s, d_conv), gates,
                    w_sb, w_cv, w_out, row(g_post_mix[l]))
        x2 = _mlp(x2, row(g_pre_mlp[l]), w_u, w_d, row(g_post_mlp[l]))
    return x2.reshape(b, s, d)
```

```python
import functools
import math

import jax
import jax.numpy as jnp
from jax import lax
from jax.experimental import pallas as pl
from jax.experimental.pallas import tpu as pltpu

F32 = jnp.float32
BF16 = jnp.bfloat16

EPS = 1e-6
LOG2E = 1.4426950408889634
HEAD_DIM = 128
SUBLANES = 8
CONV_WIDTH = 31
CONV_HALO = 32

VMEM_LIMIT_BYTES = 56 * 1024 * 1024

INPROJ_TM = 1024
INPROJ_TN = 1024
ATTN_TQ = 256
ATTN_SUB = 128
ATTN_HEADS_PER_STEP = 8
CONV_TS = 512
CONV_ROWS = 32
MERGE_TM = 512
MLP_TM = 1024
MLP_TF = 512


def _params(*sem):
    return pltpu.CompilerParams(dimension_semantics=sem, vmem_limit_bytes=VMEM_LIMIT_BYTES)


def _rms_scale(x):
    return lax.rsqrt(jnp.mean(x * x, axis=-1, keepdims=True) + EPS)


def _inproj_kernel(x_ref, g_ref, w_ref, b_ref, qkv_ref, u_ref, gates_ref, h_ref, a_ref):
    j = pl.program_id(1)

    @pl.when(j == 0)
    def _():
        x = x_ref[...]
        h_ref[...] = (x * _rms_scale(x) * g_ref[...]).astype(BF16)

    def proj():
        return jnp.dot(h_ref[...], w_ref[...], preferred_element_type=F32) + b_ref[...]

    @pl.when(j < 3)
    def _():
        qkv_ref[...] = proj().astype(BF16)

    @pl.when(j == 3)
    def _():
        a_ref[...] = proj()

    @pl.when(j == 4)
    def _():
        u_ref[...] = a_ref[...] * jax.nn.sigmoid(proj())

    @pl.when(j >= 5)
    def _():
        gates_ref[...] = jax.nn.sigmoid(proj()).astype(BF16)


def _inproj(x2, g, w, b):
    m, d = x2.shape
    tm, tn = INPROJ_TM, INPROJ_TN
    n_tiles = w.shape[1] // tn
    assert n_tiles == 9 and tn == 1024
    return pl.pallas_call(
        _inproj_kernel,
        grid=(m // tm, n_tiles),
        in_specs=[
            pl.BlockSpec((tm, d), lambda i, j: (i, 0)),
            pl.BlockSpec((1, d), lambda i, j: (0, 0)),
            pl.BlockSpec((d, tn), lambda i, j: (0, j)),
            pl.BlockSpec((1, tn), lambda i, j: (0, j)),
        ],
        out_specs=[
            pl.BlockSpec((tm, tn), lambda i, j: (i, jnp.minimum(j, 2))),
            pl.BlockSpec((tm, tn), lambda i, j: (i, 0)),
            pl.BlockSpec((tm, tn), lambda i, j: (i, jnp.clip(j - 5, 0, 3))),
        ],
        out_shape=[
            jax.ShapeDtypeStruct((m, 3 * tn), BF16),
            jax.ShapeDtypeStruct((m, tn), F32),
            jax.ShapeDtypeStruct((m, 4 * tn), BF16),
        ],
        scratch_shapes=[pltpu.VMEM((tm, d), BF16), pltpu.VMEM((tm, tn), F32)],
        compiler_params=_params("parallel", "arbitrary"),
        name="inproj",
    )(x2, g, w, b)


def _attn_kernel(q_ref, k_ref, v_ref, tri_ref, o_ref, z_a, z_b, l_a, l_b, carry_ref, acc_ref,
                 *, heads, tq, sub):
    qi = pl.program_id(2)
    scale = 1.0 / math.sqrt(HEAD_DIM)
    nsub = tq // sub
    row = lax.broadcasted_iota(jnp.int32, (tq, tq), 0)
    col = lax.broadcasted_iota(jnp.int32, (tq, tq), 1)
    causal = col < row

    def key_start(n):
        return pl.multiple_of((qi - n) * tq, tq)

    def scores(h, k0, z_buf, l_buf, diag=False):
        lanes = pl.ds(h * HEAD_DIM, HEAD_DIM)
        kb = k_ref[0, pl.ds(k0, tq), lanes]
        z = lax.dot_general(q_ref[0, :, lanes], kb, (((1,), (1,)), ((), ())),
                            preferred_element_type=F32) * scale
        z_buf[h] = z
        softplus = jnp.maximum(z, 0.0) + jnp.log(1.0 + jnp.exp2(jnp.abs(z) * (-LOG2E)))
        if diag:
            softplus = jnp.where(causal, softplus, 0.0)
        hi = softplus.astype(BF16)
        lo = (softplus - hi.astype(F32)).astype(BF16)
        for s in range(nsub):
            l_buf[h, s * tq:(s + 1) * tq, 0:sub] = hi[:, s * sub:(s + 1) * sub]
            l_buf[h, s * tq:(s + 1) * tq, sub:2 * sub] = lo[:, s * sub:(s + 1) * sub]

    def suffix_sums(h, l_buf):
        return jnp.dot(l_buf[h], tri_ref[...], preferred_element_type=F32)

    def weights_times_v(h, k0, z_buf, cs, diag=False):
        carry = jnp.zeros((tq, HEAD_DIM), F32) if diag else carry_ref[h]
        parts = [None] * nsub
        for s in reversed(range(nsub)):
            blk = cs[s * tq:(s + 1) * tq]
            parts[s] = jnp.exp(z_buf[h, :, s * sub:(s + 1) * sub] + blk[:, :sub] + carry)
            carry = carry + blk[:, sub:]
        carry_ref[h] = carry
        a = jnp.concatenate(parts, axis=1)
        if diag:
            a = jnp.where(causal, a, 0.0)
        vb = v_ref[0, pl.ds(k0, tq), pl.ds(h * HEAD_DIM, HEAD_DIM)]
        av = jnp.dot(a.astype(BF16), vb, preferred_element_type=F32)
        if diag:
            acc_ref[h] = av
        else:
            acc_ref[h] += av

    def step(nxt=None, cur=None, cur_diag=False, nxt_diag=False):
        if nxt is not None and cur is not None:
            for h in range(heads):
                scores(h, *nxt, diag=nxt_diag)
                weights_times_v(h, cur[0], cur[1], suffix_sums(h, cur[2]), diag=cur_diag)
        elif nxt is not None:
            for h in range(heads):
                scores(h, *nxt, diag=nxt_diag)
        else:
            css = [suffix_sums(h, cur[2]) for h in range(heads)]
            for h in range(heads):
                weights_times_v(h, cur[0], cur[1], css[h], diag=cur_diag)

    buf = ((z_a, l_a), (z_b, l_b))
    tile = lambda n, parity: (key_start(n),) + buf[parity]
    step(nxt=tile(0, 0), nxt_diag=True)

    @pl.when(qi == 0)
    def _():
        step(cur=tile(0, 0), cur_diag=True)

    @pl.when(qi > 0)
    def _():
        step(nxt=tile(1, 1), cur=tile(0, 0), cur_diag=True)

        def pair(p, _):
            n = 2 * p + 1
            step(nxt=tile(n + 1, 0), cur=tile(n, 1))
            step(nxt=tile(n + 2, 1), cur=tile(n + 1, 0))
            return 0

        n_pairs = (qi - 1) // 2
        lax.fori_loop(0, n_pairs, pair, 0)
        n = 2 * n_pairs + 1

        @pl.when(qi - n == 0)
        def _():
            step(cur=tile(n, 1))

        @pl.when(qi - n == 1)
        def _():
            step(nxt=tile(n + 1, 0), cur=tile(n, 1))
            step(cur=tile(n + 1, 0))

    for h in range(heads):
        o_ref[0, :, pl.ds(h * HEAD_DIM, HEAD_DIM)] = acc_ref[h].astype(o_ref.dtype)


def _attention(qkv3, n_heads):
    b, s, _ = qkv3.shape
    tq, sub, heads = ATTN_TQ, ATTN_SUB, ATTN_HEADS_PER_STEP
    width = heads * HEAD_DIM
    groups = n_heads // heads
    r = jnp.arange(2 * sub) % sub
    c = jnp.arange(2 * sub)
    tri = -jnp.where(c[None, :] < sub, r[:, None] >= c[None, :], True).astype(BF16)
    kern = functools.partial(_attn_kernel, heads=heads, tq=tq, sub=sub)
    return pl.pallas_call(
        kern,
        grid=(b, groups, s // tq),
        in_specs=[
            pl.BlockSpec((1, tq, width), lambda bi, g, qi: (bi, qi, g)),
            pl.BlockSpec((1, s, width), lambda bi, g, qi: (bi, 0, groups + g)),
            pl.BlockSpec((1, s, width), lambda bi, g, qi: (bi, 0, 2 * groups + g)),
            pl.BlockSpec((2 * sub, 2 * sub), lambda bi, g, qi: (0, 0)),
        ],
        out_specs=pl.BlockSpec((1, tq, width), lambda bi, g, qi: (bi, qi, g)),
        out_shape=jax.ShapeDtypeStruct((b, s, n_heads * HEAD_DIM), BF16),
        scratch_shapes=[
            pltpu.VMEM((heads, tq, tq), F32), pltpu.VMEM((heads, tq, tq), F32),
            pltpu.VMEM((heads, (tq // sub) * tq, 2 * sub), BF16),
            pltpu.VMEM((heads, (tq // sub) * tq, 2 * sub), BF16),
            pltpu.VMEM((heads, tq, HEAD_DIM), F32), pltpu.VMEM((heads, tq, HEAD_DIM), F32),
        ],
        compiler_params=_params("parallel", "parallel", "arbitrary"),
        name="attention",
    )(qkv3, qkv3, qkv3, tri)


def _conv_kernel(*refs, ts, rows, n_cast):
    u_ref, halo_ref, w_ref, bdw_ref, g_ref, bln_ref = refs[:6]
    cast_in = refs[6:6 + n_cast]
    o_ref = refs[6 + n_cast]
    cast_out = refs[7 + n_cast:7 + 2 * n_cast]
    xs_ref, wb_ref = refs[7 + 2 * n_cast:]

    for src, dst in zip(cast_in, cast_out):
        dst[...] = src[...].astype(dst.dtype)

    i = pl.program_id(1)
    n = ts + CONV_HALO
    xs_ref[0, 0:CONV_HALO, :] = jnp.where(i > 0, halo_ref[0], 0.0)
    xs_ref[0, CONV_HALO:, :] = u_ref[0]
    padded = xs_ref[0]
    for p in range(1, SUBLANES):
        xs_ref[p] = pltpu.roll(padded, n - p, axis=0)
    c = w_ref.shape[1]
    for j in range(CONV_WIDTH):
        wb_ref[j] = jnp.broadcast_to(w_ref[j:j + 1, :], (SUBLANES, c))
    first = CONV_HALO - (CONV_WIDTH - 1)
    for r0 in range(0, ts, rows):
        acc = None
        for j in range(CONV_WIDTH):
            p, base = (first + j) % SUBLANES, (first + j) // SUBLANES * SUBLANES
            window = xs_ref[p, pl.ds(base + r0, rows), :].reshape(rows // SUBLANES, SUBLANES, c)
            term = window * wb_ref[j]
            acc = term if acc is None else acc + term
        acc = acc.reshape(rows, c) + bdw_ref[...]
        mu = jnp.mean(acc, axis=-1, keepdims=True)
        cen = acc - mu
        var = jnp.mean(cen * cen, axis=-1, keepdims=True)
        y = cen * lax.rsqrt(var + EPS) * g_ref[...] + bln_ref[...]
        o_ref[0, pl.ds(r0, rows), :] = (y * jax.nn.sigmoid(y)).astype(o_ref.dtype)


def _conv_branch(u3, w_dw, b_dw, g_ln, b_ln, weights_f32):
    b, s, c = u3.shape
    ts, rows = CONV_TS, CONV_ROWS
    n_i = s // ts
    steps = b * n_i
    ratio = ts // CONV_HALO
    kern = functools.partial(_conv_kernel, ts=ts, rows=rows, n_cast=len(weights_f32))
    vec = pl.BlockSpec((1, c), lambda bi, i: (0, 0))
    slab = lambda w: pl.BlockSpec((w.shape[0] // steps, w.shape[1]), lambda bi, i: (bi * n_i + i, 0))
    for w in weights_f32:
        assert w.shape[0] % (steps * 16) == 0, w.shape
    outs = pl.pallas_call(
        kern,
        grid=(b, n_i),
        in_specs=[
            pl.BlockSpec((1, ts, c), lambda bi, i: (bi, i, 0)),
            pl.BlockSpec((1, CONV_HALO, c), lambda bi, i: (bi, jnp.maximum(i * ratio - 1, 0), 0)),
            pl.BlockSpec((CONV_WIDTH, c), lambda bi, i: (0, 0)),
            vec, vec, vec,
        ] + [slab(w) for w in weights_f32],
        out_specs=[pl.BlockSpec((1, ts, c), lambda bi, i: (bi, i, 0))] + [slab(w) for w in weights_f32],
        out_shape=[jax.ShapeDtypeStruct((b, s, c), BF16)]
        + [jax.ShapeDtypeStruct(w.shape, BF16) for w in weights_f32],
        scratch_shapes=[pltpu.VMEM((SUBLANES, ts + CONV_HALO, c), F32),
                        pltpu.VMEM((CONV_WIDTH, SUBLANES, c), F32)],
        compiler_params=_params("parallel", "arbitrary"),
        name="conv",
    )(u3, u3, w_dw, b_dw, g_ln, b_ln, *weights_f32)
    return outs[0], outs[1:]


def _merge_kernel(x_ref, osb_ref, ocv_ref, gsb_ref, gcv_ref, wsb_ref, wcv_ref, wo_ref, g_ref, o_ref):
    a = jnp.dot(osb_ref[...], wsb_ref[...], preferred_element_type=F32)
    c = jnp.dot(ocv_ref[...], wcv_ref[...], preferred_element_type=F32)
    merged = gsb_ref[...].astype(F32) * a + gcv_ref[...].astype(F32) * c
    y = jnp.dot(merged.astype(BF16), wo_ref[...], preferred_element_type=F32)
    o_ref[...] = x_ref[...] + y * _rms_scale(y) * g_ref[...]


def _merge(x2, osb, ocv, gates, w_sb, w_cv, w_o, g):
    m, d = x2.shape
    dh = osb.shape[1]
    tm = MERGE_TM
    resident = lambda shape: pl.BlockSpec(shape, lambda i: (0, 0), pipeline_mode=pl.Buffered(1))
    return pl.pallas_call(
        _merge_kernel,
        grid=(m // tm,),
        in_specs=[
            pl.BlockSpec((tm, d), lambda i: (i, 0)),
            pl.BlockSpec((tm, dh), lambda i: (i, 0)),
            pl.BlockSpec((tm, dh), lambda i: (i, 0)),
            pl.BlockSpec((tm, d), lambda i: (i, 0)),
            pl.BlockSpec((tm, d), lambda i: (i, 1)),
            resident((dh, d)), resident((dh, d)), resident((d, d)), resident((1, d)),
        ],
        out_specs=pl.BlockSpec((tm, d), lambda i: (i, 0)),
        out_shape=jax.ShapeDtypeStruct((m, d), F32),
        compiler_params=_params("parallel"),
        name="merge",
    )(x2, osb, ocv, gates, gates, w_sb, w_cv, w_o, g)


def _mlp_kernel(x_ref, g1_ref, wu_ref, wd_ref, g2_ref, o_ref, h_ref):
    f = pl.program_id(1)

    @pl.when(f == 0)
    def _():
        x = x_ref[...]
        h_ref[...] = (x * _rms_scale(x) * g1_ref[...]).astype(BF16)
        o_ref[...] = jnp.zeros_like(o_ref)

    t = jnp.dot(h_ref[...], wu_ref[...], preferred_element_type=F32)
    t = jnp.square(jnp.maximum(t, 0.0)).astype(BF16)
    o_ref[...] += jnp.dot(t, wd_ref[...], preferred_element_type=F32)

    @pl.when(f == pl.num_programs(1) - 1)
    def _():
        y = o_ref[...]
        o_ref[...] = x_ref[...] + y * _rms_scale(y) * g2_ref[...]


def _mlp(x2, g1, w_up, w_down, g2):
    m, d = x2.shape
    dff = w_up.shape[1]
    tm, tf = MLP_TM, MLP_TF
    vec = pl.BlockSpec((1, d), lambda i, f: (0, 0))
    return pl.pallas_call(
        _mlp_kernel,
        grid=(m // tm, dff // tf),
        in_specs=[
            pl.BlockSpec((tm, d), lambda i, f: (i, 0)),
            vec,
            pl.BlockSpec((d, tf), lambda i, f: (0, f)),
            pl.BlockSpec((tf, d), lambda i, f: (f, 0)),
            vec,
        ],
        out_specs=pl.BlockSpec((tm, d), lambda i, f: (i, 0)),
        out_shape=jax.ShapeDtypeStruct((m, d), F32),
        scratch_shapes=[pltpu.VMEM((tm, d), BF16)],
        compiler_params=_params("parallel", "arbitrary"),
        name="mlp",
    )(x2, g1, w_up, w_down, g2)


def kernel(x, g_pre_mix, w_in, b_in, w_dw, b_dw, g_conv_ln, b_conv_ln, w_sb_out, w_conv_out, w_o,
           g_post_mix, g_pre_mlp, w_up, w_down, g_post_mlp):
    b, s, d = x.shape
    d_sb = w_sb_out.shape[1]
    d_conv = w_dw.shape[2]
    n_heads = d_sb // HEAD_DIM
    row = lambda v: v.reshape(1, -1)
    x2 = x.reshape(b * s, d)
    for l in range(w_in.shape[0]):
        qkv, u, gates = _inproj(x2, row(g_pre_mix[l]), w_in[l].astype(BF16), row(b_in[l]))
        o_sb = _attention(qkv.reshape(b, s, 3 * d_sb), n_heads)
        o_cv, (w_sb, w_cv, w_out, w_u, w_d) = _conv_branch(
            u.reshape(b, s, d_conv), w_dw[l], row(b_dw[l]), row(g_conv_ln[l]), row(b_conv_ln[l]),
            (w_sb_out[l], w_conv_out[l], w_o[l], w_up[l], w_down[l]))
        x2 = _merge(x2, o_sb.reshape(b * s, d_sb), o_cv.reshape(b * s, d_conv), gates,
                    w_sb, w_cv, w_out, row(g_post_mix[l]))
        x2 = _mlp(x2, row(g_pre_mlp[l]), w_u, w_d, row(g_post_mlp[l]))
    return x2.reshape(b, s, d)
```

```python
import functools
import math

import jax
import jax.numpy as jnp
from jax import lax
from jax.experimental import pallas as pl
from jax.experimental.pallas import tpu as pltpu

F32 = jnp.float32
BF16 = jnp.bfloat16

EPS = 1e-6
LOG2E = 1.4426950408889634
HEAD_DIM = 128
SUBLANES = 8
CONV_WIDTH = 31
CONV_HALO = 32

VMEM_LIMIT_BYTES = 56 * 1024 * 1024

INPROJ_TM = 1024
INPROJ_TN = 1024
ATTN_TQ = 256
ATTN_SUB = 128
ATTN_HEADS_PER_STEP = 8
CONV_TS = 256
CONV_ROWS = 32
MERGE_TM = 512
MLP_TM = 512
MLP_TF = 1024


def _params(*sem):
    return pltpu.CompilerParams(dimension_semantics=sem, vmem_limit_bytes=VMEM_LIMIT_BYTES)


def _rms_scale(x):
    return lax.rsqrt(jnp.mean(x * x, axis=-1, keepdims=True) + EPS)


def _inproj_kernel(x_ref, g_ref, w_ref, b_ref, qkv_ref, u_ref, gates_ref, h_ref, a_ref):
    j = pl.program_id(1)

    @pl.when(j == 0)
    def _():
        x = x_ref[...]
        h_ref[...] = (x * _rms_scale(x) * g_ref[...]).astype(BF16)

    def proj():
        return jnp.dot(h_ref[...], w_ref[...], preferred_element_type=F32) + b_ref[...]

    @pl.when(j < 3)
    def _():
        qkv_ref[...] = proj().astype(BF16)

    @pl.when(j == 3)
    def _():
        a_ref[...] = proj()

    @pl.when(j == 4)
    def _():
        u_ref[...] = a_ref[...] * jax.nn.sigmoid(proj())

    @pl.when(j >= 5)
    def _():
        gates_ref[...] = jax.nn.sigmoid(proj()).astype(BF16)


def _inproj(x2, g, w, b):
    m, d = x2.shape
    tm, tn = INPROJ_TM, INPROJ_TN
    n_tiles = w.shape[1] // tn
    assert n_tiles == 9 and tn == 1024
    return pl.pallas_call(
        _inproj_kernel,
        grid=(m // tm, n_tiles),
        in_specs=[
            pl.BlockSpec((tm, d), lambda i, j: (i, 0)),
            pl.BlockSpec((1, d), lambda i, j: (0, 0)),
            pl.BlockSpec((d, tn), lambda i, j: (0, j)),
            pl.BlockSpec((1, tn), lambda i, j: (0, j)),
        ],
        out_specs=[
            pl.BlockSpec((tm, tn), lambda i, j: (i, jnp.minimum(j, 2))),
            pl.BlockSpec((tm, tn), lambda i, j: (i, 0)),
            pl.BlockSpec((tm, tn), lambda i, j: (i, jnp.clip(j - 5, 0, 3))),
        ],
        out_shape=[
            jax.ShapeDtypeStruct((m, 3 * tn), BF16),
            jax.ShapeDtypeStruct((m, tn), F32),
            jax.ShapeDtypeStruct((m, 4 * tn), BF16),
        ],
        scratch_shapes=[pltpu.VMEM((tm, d), BF16), pltpu.VMEM((tm, tn), F32)],
        compiler_params=_params("parallel", "arbitrary"),
        name="inproj",
    )(x2, g, w, b)


def _attn_kernel(q_ref, k_ref, v_ref, tri_ref, o_ref, z_a, z_b, l_a, l_b, carry_ref, acc_ref, kt_ref,
                 *, heads, tq, sub):
    qi = pl.program_id(2)

    @pl.when(qi == 0)
    def _():
        for h in range(heads):
            for r0 in range(0, k_ref.shape[1], tq):
                kt_ref[h, :, r0:r0 + tq] = k_ref[0, r0:r0 + tq, pl.ds(h * HEAD_DIM, HEAD_DIM)].T
    scale = 1.0 / math.sqrt(HEAD_DIM)
    nsub = tq // sub
    row = lax.broadcasted_iota(jnp.int32, (tq, tq), 0)
    col = lax.broadcasted_iota(jnp.int32, (tq, tq), 1)
    causal = col < row

    def key_start(n):
        return pl.multiple_of((qi - n) * tq, tq)

    def scores(h, k0, z_buf, l_buf, diag=False):
        lanes = pl.ds(h * HEAD_DIM, HEAD_DIM)
        z = jnp.dot(q_ref[0, :, lanes], kt_ref[h, :, pl.ds(k0, tq)], preferred_element_type=F32) * scale
        z_buf[h] = z
        softplus = jnp.maximum(z, 0.0) + jnp.log(1.0 + jnp.exp2(jnp.abs(z) * (-LOG2E)))
        if diag:
            softplus = jnp.where(causal, softplus, 0.0)
        hi = softplus.astype(BF16)
        lo = (softplus - hi.astype(F32)).astype(BF16)
        for s in range(nsub):
            l_buf[h, s * tq:(s + 1) * tq, 0:sub] = hi[:, s * sub:(s + 1) * sub]
            l_buf[h, s * tq:(s + 1) * tq, sub:2 * sub] = lo[:, s * sub:(s + 1) * sub]

    def suffix_sums(h, l_buf):
        return jnp.dot(l_buf[h], tri_ref[...], preferred_element_type=F32)

    def weights_times_v(h, k0, z_buf, cs, diag=False):
        carry = jnp.zeros((tq, HEAD_DIM), F32) if diag else carry_ref[h]
        parts = [None] * nsub
        for s in reversed(range(nsub)):
            blk = cs[s * tq:(s + 1) * tq]
            parts[s] = jnp.exp(z_buf[h, :, s * sub:(s + 1) * sub] + blk[:, :sub] + carry)
            carry = carry + blk[:, sub:]
        carry_ref[h] = carry
        a = jnp.concatenate(parts, axis=1)
        if diag:
            a = jnp.where(causal, a, 0.0)
        vb = v_ref[0, pl.ds(k0, tq), pl.ds(h * HEAD_DIM, HEAD_DIM)]
        av = jnp.dot(a.astype(BF16), vb, preferred_element_type=F32)
        if diag:
            acc_ref[h] = av
        else:
            acc_ref[h] += av

    def step(nxt=None, cur=None, cur_diag=False, nxt_diag=False):
        if nxt is not None and cur is not None:
            for h in range(heads):
                scores(h, *nxt, diag=nxt_diag)
                weights_times_v(h, cur[0], cur[1], suffix_sums(h, cur[2]), diag=cur_diag)
        elif nxt is not None:
            for h in range(heads):
                scores(h, *nxt, diag=nxt_diag)
        else:
            css = [suffix_sums(h, cur[2]) for h in range(heads)]
            for h in range(heads):
                weights_times_v(h, cur[0], cur[1], css[h], diag=cur_diag)

    buf = ((z_a, l_a), (z_b, l_b))
    tile = lambda n, parity: (key_start(n),) + buf[parity]
    step(nxt=tile(0, 0), nxt_diag=True)

    @pl.when(qi == 0)
    def _():
        step(cur=tile(0, 0), cur_diag=True)

    @pl.when(qi > 0)
    def _():
        step(nxt=tile(1, 1), cur=tile(0, 0), cur_diag=True)

        def pair(p, _):
            n = 2 * p + 1
            step(nxt=tile(n + 1, 0), cur=tile(n, 1))
            step(nxt=tile(n + 2, 1), cur=tile(n + 1, 0))
            return 0

        n_pairs = (qi - 1) // 2
        lax.fori_loop(0, n_pairs, pair, 0)
        n = 2 * n_pairs + 1

        @pl.when(qi - n == 0)
        def _():
            step(cur=tile(n, 1))

        @pl.when(qi - n == 1)
        def _():
            step(nxt=tile(n + 1, 0), cur=tile(n, 1))
            step(cur=tile(n + 1, 0))

    for h in range(heads):
        o_ref[0, :, pl.ds(h * HEAD_DIM, HEAD_DIM)] = acc_ref[h].astype(o_ref.dtype)


def _attention(qkv3, n_heads):
    b, s, _ = qkv3.shape
    tq, sub, heads = ATTN_TQ, ATTN_SUB, ATTN_HEADS_PER_STEP
    width = heads * HEAD_DIM
    groups = n_heads // heads
    r = jnp.arange(2 * sub) % sub
    c = jnp.arange(2 * sub)
    tri = -jnp.where(c[None, :] < sub, r[:, None] >= c[None, :], True).astype(BF16)
    kern = functools.partial(_attn_kernel, heads=heads, tq=tq, sub=sub)
    return pl.pallas_call(
        kern,
        grid=(b, groups, s // tq),
        in_specs=[
            pl.BlockSpec((1, tq, width), lambda bi, g, qi: (bi, qi, g)),
            pl.BlockSpec((1, s, width), lambda bi, g, qi: (bi, 0, groups + g)),
            pl.BlockSpec((1, s, width), lambda bi, g, qi: (bi, 0, 2 * groups + g)),
            pl.BlockSpec((2 * sub, 2 * sub), lambda bi, g, qi: (0, 0)),
        ],
        out_specs=pl.BlockSpec((1, tq, width), lambda bi, g, qi: (bi, qi, g)),
        out_shape=jax.ShapeDtypeStruct((b, s, n_heads * HEAD_DIM), BF16),
        scratch_shapes=[
            pltpu.VMEM((heads, tq, tq), F32), pltpu.VMEM((heads, tq, tq), F32),
            pltpu.VMEM((heads, (tq // sub) * tq, 2 * sub), BF16),
            pltpu.VMEM((heads, (tq // sub) * tq, 2 * sub), BF16),
            pltpu.VMEM((heads, tq, HEAD_DIM), F32), pltpu.VMEM((heads, tq, HEAD_DIM), F32),
            pltpu.VMEM((heads, HEAD_DIM, s), BF16),
        ],
        compiler_params=_params("parallel", "parallel", "arbitrary"),
        name="attention",
    )(qkv3, qkv3, qkv3, tri)


def _conv_kernel(*refs, ts, rows, n_cast):
    u_ref, halo_ref, w_ref, bdw_ref, g_ref, bln_ref = refs[:6]
    cast_in = refs[6:6 + n_cast]
    o_ref = refs[6 + n_cast]
    cast_out = refs[7 + n_cast:7 + 2 * n_cast]
    xs_ref, wb_ref = refs[7 + 2 * n_cast:]

    for src, dst in zip(cast_in, cast_out):
        dst[...] = src[...].astype(dst.dtype)

    i = pl.program_id(1)
    n = ts + CONV_HALO
    xs_ref[0, 0:CONV_HALO, :] = jnp.where(i > 0, halo_ref[0], 0.0)
    xs_ref[0, CONV_HALO:, :] = u_ref[0]
    padded = xs_ref[0]
    for p in range(1, SUBLANES):
        xs_ref[p] = pltpu.roll(padded, n - p, axis=0)
    c = w_ref.shape[1]
    for j in range(CONV_WIDTH):
        wb_ref[j] = jnp.broadcast_to(w_ref[j:j + 1, :], (SUBLANES, c))
    first = CONV_HALO - (CONV_WIDTH - 1)
    for r0 in range(0, ts, rows):
        acc = None
        for j in range(CONV_WIDTH):
            p, base = (first + j) % SUBLANES, (first + j) // SUBLANES * SUBLANES
            window = xs_ref[p, pl.ds(base + r0, rows), :].reshape(rows // SUBLANES, SUBLANES, c)
            term = window * wb_ref[j]
            acc = term if acc is None else acc + term
        acc = acc.reshape(rows, c) + bdw_ref[...]
        mu = jnp.mean(acc, axis=-1, keepdims=True)
        cen = acc - mu
        var = jnp.mean(cen * cen, axis=-1, keepdims=True)
        y = cen * lax.rsqrt(var + EPS) * g_ref[...] + bln_ref[...]
        o_ref[0, pl.ds(r0, rows), :] = (y * jax.nn.sigmoid(y)).astype(o_ref.dtype)


def _conv_branch(u3, w_dw, b_dw, g_ln, b_ln, weights_f32):
    b, s, c = u3.shape
    ts, rows = CONV_TS, CONV_ROWS
    n_i = s // ts
    steps = b * n_i
    ratio = ts // CONV_HALO
    kern = functools.partial(_conv_kernel, ts=ts, rows=rows, n_cast=len(weights_f32))
    vec = pl.BlockSpec((1, c), lambda bi, i: (0, 0))
    slab = lambda w: pl.BlockSpec((w.shape[0] // steps, w.shape[1]), lambda bi, i: (bi * n_i + i, 0))
    for w in weights_f32:
        assert w.shape[0] % (steps * 16) == 0, w.shape
    outs = pl.pallas_call(
        kern,
        grid=(b, n_i),
        in_specs=[
            pl.BlockSpec((1, ts, c), lambda bi, i: (bi, i, 0)),
            pl.BlockSpec((1, CONV_HALO, c), lambda bi, i: (bi, jnp.maximum(i * ratio - 1, 0), 0)),
            pl.BlockSpec((CONV_WIDTH, c), lambda bi, i: (0, 0)),
            vec, vec, vec,
        ] + [slab(w) for w in weights_f32],
        out_specs=[pl.BlockSpec((1, ts, c), lambda bi, i: (bi, i, 0))] + [slab(w) for w in weights_f32],
        out_shape=[jax.ShapeDtypeStruct((b, s, c), BF16)]
        + [jax.ShapeDtypeStruct(w.shape, BF16) for w in weights_f32],
        scratch_shapes=[pltpu.VMEM((SUBLANES, ts + CONV_HALO, c), F32),
                        pltpu.VMEM((CONV_WIDTH, SUBLANES, c), F32)],
        compiler_params=_params("parallel", "arbitrary"),
        name="conv",
    )(u3, u3, w_dw, b_dw, g_ln, b_ln, *weights_f32)
    return outs[0], outs[1:]


def _merge_kernel(x_ref, osb_ref, ocv_ref, gsb_ref, gcv_ref, wsb_ref, wcv_ref, wo_ref, g_ref, o_ref):
    a = jnp.dot(osb_ref[...], wsb_ref[...], preferred_element_type=F32)
    c = jnp.dot(ocv_ref[...], wcv_ref[...], preferred_element_type=F32)
    merged = gsb_ref[...].astype(F32) * a + gcv_ref[...].astype(F32) * c
    y = jnp.dot(merged.astype(BF16), wo_ref[...], preferred_element_type=F32)
    o_ref[...] = x_ref[...] + y * _rms_scale(y) * g_ref[...]


def _merge(x2, osb, ocv, gates, w_sb, w_cv, w_o, g):
    m, d = x2.shape
    dh = osb.shape[1]
    tm = MERGE_TM
    resident = lambda shape: pl.BlockSpec(shape, lambda i: (0, 0), pipeline_mode=pl.Buffered(1))
    return pl.pallas_call(
        _merge_kernel,
        grid=(m // tm,),
        in_specs=[
            pl.BlockSpec((tm, d), lambda i: (i, 0)),
            pl.BlockSpec((tm, dh), lambda i: (i, 0)),
            pl.BlockSpec((tm, dh), lambda i: (i, 0)),
            pl.BlockSpec((tm, d), lambda i: (i, 0)),
            pl.BlockSpec((tm, d), lambda i: (i, 1)),
            resident((dh, d)), resident((dh, d)), resident((d, d)), resident((1, d)),
        ],
        out_specs=pl.BlockSpec((tm, d), lambda i: (i, 0)),
        out_shape=jax.ShapeDtypeStruct((m, d), F32),
        compiler_params=_params("parallel"),
        name="merge",
    )(x2, osb, ocv, gates, gates, w_sb, w_cv, w_o, g)


def _mlp_kernel(x_ref, g1_ref, wu_ref, wd_ref, g2_ref, o_ref, h_ref):
    f = pl.program_id(1)

    @pl.when(f == 0)
    def _():
        x = x_ref[...]
        h_ref[...] = (x * _rms_scale(x) * g1_ref[...]).astype(BF16)
        o_ref[...] = jnp.zeros_like(o_ref)

    t = jnp.dot(h_ref[...], wu_ref[...], preferred_element_type=F32)
    t = jnp.square(jnp.maximum(t, 0.0)).astype(BF16)
    o_ref[...] += jnp.dot(t, wd_ref[...], preferred_element_type=F32)

    @pl.when(f == pl.num_programs(1) - 1)
    def _():
        y = o_ref[...]
        o_ref[...] = x_ref[...] + y * _rms_scale(y) * g2_ref[...]


def _mlp(x2, g1, w_up, w_down, g2):
    m, d = x2.shape
    dff = w_up.shape[1]
    tm, tf = MLP_TM, MLP_TF
    vec = pl.BlockSpec((1, d), lambda i, f: (0, 0))
    return pl.pallas_call(
        _mlp_kernel,
        grid=(m // tm, dff // tf),
        in_specs=[
            pl.BlockSpec((tm, d), lambda i, f: (i, 0)),
            vec,
            pl.BlockSpec((d, tf), lambda i, f: (0, f)),
            pl.BlockSpec((tf, d), lambda i, f: (f, 0)),
            vec,
        ],
        out_specs=pl.BlockSpec((tm, d), lambda i, f: (i, 0)),
        out_shape=jax.ShapeDtypeStruct((m, d), F32),
        scratch_shapes=[pltpu.VMEM((tm, d), BF16)],
        compiler_params=_params("parallel", "arbitrary"),
        name="mlp",
    )(x2, g1, w_up, w_down, g2)


def kernel(x, g_pre_mix, w_in, b_in, w_dw, b_dw, g_conv_ln, b_conv_ln, w_sb_out, w_conv_out, w_o,
           g_post_mix, g_pre_mlp, w_up, w_down, g_post_mlp):
    b, s, d = x.shape
    d_sb = w_sb_out.shape[1]
    d_conv = w_dw.shape[2]
    n_heads = d_sb // HEAD_DIM
    row = lambda v: v.reshape(1, -1)
    x2 = x.reshape(b * s, d)
    for l in range(w_in.shape[0]):
        qkv, u, gates = _inproj(x2, row(g_pre_mix[l]), w_in[l].astype(BF16), row(b_in[l]))
        o_sb = _attention(qkv.reshape(b, s, 3 * d_sb), n_heads)
        o_cv, (w_sb, w_cv, w_out, w_u, w_d) = _conv_branch(
            u.reshape(b, s, d_conv), w_dw[l], row(b_dw[l]), row(g_conv_ln[l]), row(b_conv_ln[l]),
            (w_sb_out[l], w_conv_out[l], w_o[l], w_up[l], w_down[l]))
        x2 = _merge(x2, o_sb.reshape(b * s, d_sb), o_cv.reshape(b * s, d_conv), gates,
                    w_sb, w_cv, w_out, row(g_post_mix[l]))
        x2 = _mlp(x2, row(g_pre_mlp[l]), w_u, w_d, row(g_post_mlp[l]))
    return x2.reshape(b, s, d)
```

```python
import functools
import math

import jax
import jax.numpy as jnp
from jax import lax
from jax.experimental import pallas as pl
from jax.experimental.pallas import tpu as pltpu

F32 = jnp.float32
BF16 = jnp.bfloat16

EPS = 1e-6
LOG2E = 1.4426950408889634
HEAD_DIM = 128
Q_SCALE = 1.0 / math.sqrt(HEAD_DIM)
SUBLANES = 8
BF16_TILE_ROWS = 16
CONV_WIDTH = 31
CONV_HALO = 32

VMEM_LIMIT_BYTES = 56 * 1024 * 1024

INPROJ_TM = 1024
INPROJ_TN = 1024
ATTN_TQ = 256
ATTN_SUB = 128
ATTN_HEADS_PER_STEP = 8
CONV_TS = 256
CONV_ROWS = 32
MERGE_TM = 512
MLP_TM = 512
MLP_TF = 1024


def _params(*sem, flags=None):
    return pltpu.CompilerParams(dimension_semantics=sem, vmem_limit_bytes=VMEM_LIMIT_BYTES, flags=flags)


def _sigmoid(x):
    return 0.5 * jnp.tanh(0.5 * x) + 0.5


def _rms_scale(x):
    return lax.rsqrt(jnp.mean(x * x, axis=-1, keepdims=True) + EPS)


def _inproj_kernel(x_ref, g_ref, w_ref, b_ref, qkv_ref, u_ref, gates_ref, h_ref, a_ref):
    j = pl.program_id(1)

    def proj(h=None):
        h = h_ref[...] if h is None else h
        return jnp.dot(h, w_ref[...], preferred_element_type=F32) + b_ref[...]

    @pl.when(j == 0)
    def _():
        x = x_ref[...]
        h = (x * _rms_scale(x) * g_ref[...]).astype(BF16)
        h_ref[...] = h
        qkv_ref[...] = (proj(h) * Q_SCALE).astype(BF16)

    @pl.when(jnp.logical_and(j > 0, j < 3))
    def _():
        qkv_ref[...] = proj().astype(BF16)

    @pl.when(j == 3)
    def _():
        a_ref[...] = proj()

    @pl.when(j == 4)
    def _():
        u_ref[...] = a_ref[...] * _sigmoid(proj())

    @pl.when(j >= 5)
    def _():
        gates_ref[...] = _sigmoid(proj()).astype(BF16)


def _inproj(x2, g, w, b):
    m, d = x2.shape
    tm, tn = INPROJ_TM, INPROJ_TN
    n_tiles = w.shape[1] // tn
    assert n_tiles == 9 and d == 2 * tn
    return pl.pallas_call(
        _inproj_kernel,
        grid=(m // tm, n_tiles),
        in_specs=[
            pl.BlockSpec((tm, d), lambda i, j: (i, 0)),
            pl.BlockSpec((1, d), lambda i, j: (0, 0)),
            pl.BlockSpec((d, tn), lambda i, j: (0, j)),
            pl.BlockSpec((1, tn), lambda i, j: (0, j)),
        ],
        out_specs=[
            pl.BlockSpec((tm, tn), lambda i, j: (i, jnp.minimum(j, 2))),
            pl.BlockSpec((tm, tn), lambda i, j: (i, 0)),
            pl.BlockSpec((tm, tn), lambda i, j: (i, jnp.clip(j - 5, 0, 3))),
        ],
        out_shape=[
            jax.ShapeDtypeStruct((m, 3 * tn), BF16),
            jax.ShapeDtypeStruct((m, tn), F32),
            jax.ShapeDtypeStruct((m, 4 * tn), BF16),
        ],
        scratch_shapes=[pltpu.VMEM((tm, d), BF16), pltpu.VMEM((tm, tn), F32)],
        compiler_params=_params("parallel", "arbitrary"),
        name="inproj",
    )(x2, g, w, b)


def _attn_kernel(q_ref, k_ref, v_ref, tri_ref, o_ref, z_a, z_b, l_a, l_b, carry_ref, acc_ref,
                 *, heads, tq, sub):
    qi = pl.program_id(2)
    nsub = tq // sub
    row = lax.broadcasted_iota(jnp.int32, (sub, sub), 0)
    col = lax.broadcasted_iota(jnp.int32, (sub, sub), 1)
    causal = col < row

    def key_start(n):
        return pl.multiple_of((qi - n) * tq, tq)

    def scores(h, k0, z_buf, l_buf, diag=False):
        lanes = pl.ds(h * HEAD_DIM, HEAD_DIM)
        kb = k_ref[0, pl.ds(k0, tq), lanes]
        z = lax.dot_general(q_ref[0, :, lanes], kb, (((1,), (1,)), ((), ())),
                            preferred_element_type=F32)
        z_buf[h] = z

        def split(zz, mask=None):
            softplus = jnp.maximum(zz, 0.0) + jnp.log(1.0 + jnp.exp2(jnp.abs(zz) * (-LOG2E)))
            if mask is not None:
                softplus = jnp.where(mask, softplus, 0.0)
            hi = softplus.astype(BF16)
            return hi, (softplus - hi.astype(F32)).astype(BF16)

        if not diag:
            hi, lo = split(z)
            for s in range(nsub):
                l_buf[h, s * tq:(s + 1) * tq, 0:sub] = hi[:, s * sub:(s + 1) * sub]
                l_buf[h, s * tq:(s + 1) * tq, sub:2 * sub] = lo[:, s * sub:(s + 1) * sub]
            return
        for s in range(nsub):
            for r in range(nsub):
                rows = slice(s * tq + r * sub, s * tq + (r + 1) * sub)
                if s > r:
                    l_buf[h, rows, :] = jnp.zeros((sub, 2 * sub), BF16)
                    continue
                hi, lo = split(z[r * sub:(r + 1) * sub, s * sub:(s + 1) * sub], causal if s == r else None)
                l_buf[h, rows, 0:sub] = hi
                l_buf[h, rows, sub:2 * sub] = lo

    def suffix_sums(h, l_buf):
        return jnp.dot(l_buf[h], tri_ref[...], preferred_element_type=F32)

    def weights_times_v(h, k0, z_buf, cs, diag=False, final=False):
        if not diag:
            carry = carry_ref[h]
            parts = [None] * nsub
            for s in reversed(range(nsub)):
                blk = cs[s * tq:(s + 1) * tq]
                parts[s] = jnp.exp(z_buf[h, :, s * sub:(s + 1) * sub] + blk[:, :sub] + carry)
                carry = carry + blk[:, sub:]
            if not final:
                carry_ref[h] = carry
            a = jnp.concatenate(parts, axis=1)
        else:
            carries = [jnp.zeros((sub, HEAD_DIM), F32)] * nsub
            blocks = [[None] * nsub for _ in range(nsub)]
            for s in reversed(range(nsub)):
                for r in range(nsub):
                    if s > r:
                        blocks[r][s] = jnp.zeros((sub, sub), F32)
                        continue
                    blk = cs[s * tq + r * sub:s * tq + (r + 1) * sub]
                    e = z_buf[h, r * sub:(r + 1) * sub, s * sub:(s + 1) * sub] + blk[:, :sub] + carries[r]
                    blocks[r][s] = jnp.where(causal, jnp.exp(e), 0.0) if s == r else jnp.exp(e)
                    carries[r] = carries[r] + blk[:, sub:]
            if not final:
                carry_ref[h] = jnp.concatenate(carries, axis=0)
            a = jnp.concatenate([jnp.concatenate(row_blocks, axis=1) for row_blocks in blocks], axis=0)
        vb = v_ref[0, pl.ds(k0, tq), pl.ds(h * HEAD_DIM, HEAD_DIM)]
        av = jnp.dot(a.astype(BF16), vb, preferred_element_type=F32)
        if final:
            out = av if diag else acc_ref[h] + av
            o_ref[0, :, pl.ds(h * HEAD_DIM, HEAD_DIM)] = out.astype(o_ref.dtype)
        elif diag:
            acc_ref[h] = av
        else:
            acc_ref[h] += av

    def step(nxt=None, cur=None, cur_diag=False, nxt_diag=False):
        if nxt is not None and cur is not None:
            for h in range(heads):
                scores(h, *nxt, diag=nxt_diag)
                weights_times_v(h, cur[0], cur[1], suffix_sums(h, cur[2]), diag=cur_diag)
        elif nxt is not None:
            for h in range(heads):
                scores(h, *nxt, diag=nxt_diag)
        else:
            css = [suffix_sums(h, cur[2]) for h in range(heads)]
            for h in range(heads):
                weights_times_v(h, cur[0], cur[1], css[h], diag=cur_diag, final=True)

    buf = ((z_a, l_a), (z_b, l_b))
    tile = lambda n, parity: (key_start(n),) + buf[parity]

    @pl.when(qi == 0)
    def _():
        step(nxt=tile(0, 0), nxt_diag=True)
        step(cur=tile(0, 0), cur_diag=True)

    @pl.when(qi > 0)
    def _():
        step(nxt=tile(0, 0), nxt_diag=True)
        step(nxt=tile(1, 1), cur=tile(0, 0), cur_diag=True)

        def pair(p, _):
            n = 2 * p + 1
            step(nxt=tile(n + 1, 0), cur=tile(n, 1))
            step(nxt=tile(n + 2, 1), cur=tile(n + 1, 0))
            return 0

        n_pairs = (qi - 1) // 2
        lax.fori_loop(0, n_pairs, pair, 0)
        n = 2 * n_pairs + 1

        @pl.when(qi - n == 0)
        def _():
            step(cur=tile(n, 1))

        @pl.when(qi - n == 1)
        def _():
            step(nxt=tile(n + 1, 0), cur=tile(n, 1))
            step(cur=tile(n + 1, 0))


def _attention(qkv3, n_heads):
    b, s, _ = qkv3.shape
    tq, sub, heads = ATTN_TQ, ATTN_SUB, ATTN_HEADS_PER_STEP
    width = heads * HEAD_DIM
    groups = n_heads // heads
    r = jnp.arange(2 * sub) % sub
    c = jnp.arange(2 * sub)
    tri = -jnp.where(c[None, :] < sub, r[:, None] >= c[None, :], True).astype(BF16)
    kern = functools.partial(_attn_kernel, heads=heads, tq=tq, sub=sub)
    return pl.pallas_call(
        kern,
        grid=(b, groups, s // tq),
        in_specs=[
            pl.BlockSpec((1, tq, width), lambda bi, g, qi: (bi, qi, g)),
            pl.BlockSpec((1, s, width), lambda bi, g, qi: (bi, 0, groups + g)),
            pl.BlockSpec((1, s, width), lambda bi, g, qi: (bi, 0, 2 * groups + g)),
            pl.BlockSpec((2 * sub, 2 * sub), lambda bi, g, qi: (0, 0)),
        ],
        out_specs=pl.BlockSpec((1, tq, width), lambda bi, g, qi: (bi, qi, g)),
        out_shape=jax.ShapeDtypeStruct((b, s, n_heads * HEAD_DIM), BF16),
        scratch_shapes=[
            pltpu.VMEM((heads, tq, tq), F32), pltpu.VMEM((heads, tq, tq), F32),
            pltpu.VMEM((heads, (tq // sub) * tq, 2 * sub), BF16),
            pltpu.VMEM((heads, (tq // sub) * tq, 2 * sub), BF16),
            pltpu.VMEM((heads, tq, HEAD_DIM), F32), pltpu.VMEM((heads, tq, HEAD_DIM), F32),
        ],
        compiler_params=_params("parallel", "parallel", "arbitrary"),
        name="attention",
    )(qkv3, qkv3, qkv3, tri)


def _conv_kernel(*refs, ts, rows, n_cast):
    u_ref, halo_ref, w_ref, bdw_ref, g_ref, bln_ref = refs[:6]
    cast_in = refs[6:6 + n_cast]
    o_ref = refs[6 + n_cast]
    cast_out = refs[7 + n_cast:7 + 2 * n_cast]
    xs_ref, wb_ref = refs[7 + 2 * n_cast:]

    for src, dst in zip(cast_in, cast_out):
        dst[...] = src[...].astype(dst.dtype)

    i = pl.program_id(1)
    n = ts + CONV_HALO
    xs_ref[0, 0:CONV_HALO, :] = jnp.where(i > 0, halo_ref[0], 0.0)
    xs_ref[0, CONV_HALO:, :] = u_ref[0]
    padded = xs_ref[0]
    for p in range(1, SUBLANES):
        xs_ref[p] = pltpu.roll(padded, n - p, axis=0)
    c = w_ref.shape[1]
    for j in range(CONV_WIDTH):
        wb_ref[j] = jnp.broadcast_to(w_ref[j:j + 1, :], (SUBLANES, c))
    first = CONV_HALO - (CONV_WIDTH - 1)
    for r0 in range(0, ts, rows):
        acc = None
        for j in range(CONV_WIDTH):
            p, base = (first + j) % SUBLANES, (first + j) // SUBLANES * SUBLANES
            window = xs_ref[p, pl.ds(base + r0, rows), :].reshape(rows // SUBLANES, SUBLANES, c)
            term = window * wb_ref[j]
            acc = term if acc is None else acc + term
        acc = acc.reshape(rows, c) + bdw_ref[...]
        mu = jnp.mean(acc, axis=-1, keepdims=True)
        cen = acc - mu
        var = jnp.mean(cen * cen, axis=-1, keepdims=True)
        y = cen * lax.rsqrt(var + EPS) * g_ref[...] + bln_ref[...]
        o_ref[0, pl.ds(r0, rows), :] = (y * _sigmoid(y)).astype(o_ref.dtype)


def _conv_branch(u3, w_dw, b_dw, g_ln, b_ln, weights_f32):
    b, s, c = u3.shape
    ts, rows = CONV_TS, CONV_ROWS
    n_i = s // ts
    steps = b * n_i
    ratio = ts // CONV_HALO
    kern = functools.partial(_conv_kernel, ts=ts, rows=rows, n_cast=len(weights_f32))
    vec = pl.BlockSpec((1, c), lambda bi, i: (0, 0))
    slab = lambda w: pl.BlockSpec((w.shape[0] // steps, w.shape[1]), lambda bi, i: (bi * n_i + i, 0))
    for w in weights_f32:
        assert w.shape[0] % (steps * BF16_TILE_ROWS) == 0, w.shape
    outs = pl.pallas_call(
        kern,
        grid=(b, n_i),
        in_specs=[
            pl.BlockSpec((1, ts, c), lambda bi, i: (bi, i, 0)),
            pl.BlockSpec((1, CONV_HALO, c), lambda bi, i: (bi, jnp.maximum(i * ratio - 1, 0), 0)),
            pl.BlockSpec((CONV_WIDTH, c), lambda bi, i: (0, 0)),
            vec, vec, vec,
        ] + [slab(w) for w in weights_f32],
        out_specs=[pl.BlockSpec((1, ts, c), lambda bi, i: (bi, i, 0))] + [slab(w) for w in weights_f32],
        out_shape=[jax.ShapeDtypeStruct((b, s, c), BF16)]
        + [jax.ShapeDtypeStruct(w.shape, BF16) for w in weights_f32],
        scratch_shapes=[pltpu.VMEM((SUBLANES, ts + CONV_HALO, c), F32),
                        pltpu.VMEM((CONV_WIDTH, SUBLANES, c), F32)],
        compiler_params=_params("parallel", "arbitrary"),
        name="conv",
    )(u3, u3, w_dw, b_dw, g_ln, b_ln, *weights_f32)
    return outs[0], outs[1:]


def _merge_kernel(x_ref, osb_ref, ocv_ref, gsb_ref, gcv_ref, wsb_ref, wcv_ref, wo_ref, g_ref, o_ref):
    a = jnp.dot(osb_ref[...], wsb_ref[...], preferred_element_type=F32)
    c = jnp.dot(ocv_ref[...], wcv_ref[...], preferred_element_type=F32)
    merged = gsb_ref[...].astype(F32) * a + gcv_ref[...].astype(F32) * c
    y = jnp.dot(merged.astype(BF16), wo_ref[...], preferred_element_type=F32)
    o_ref[...] = x_ref[...] + y * _rms_scale(y) * g_ref[...]


def _merge(x2, osb, ocv, gates, w_sb, w_cv, w_o, g):
    m, d = x2.shape
    dh = osb.shape[1]
    tm = MERGE_TM
    resident = lambda shape: pl.BlockSpec(shape, lambda i: (0, 0), pipeline_mode=pl.Buffered(1))
    return pl.pallas_call(
        _merge_kernel,
        grid=(m // tm,),
        in_specs=[
            pl.BlockSpec((tm, d), lambda i: (i, 0)),
            pl.BlockSpec((tm, dh), lambda i: (i, 0)),
            pl.BlockSpec((tm, dh), lambda i: (i, 0)),
            pl.BlockSpec((tm, d), lambda i: (i, 0)),
            pl.BlockSpec((tm, d), lambda i: (i, 1)),
            resident((dh, d)), resident((dh, d)), resident((d, d)), resident((1, d)),
        ],
        out_specs=pl.BlockSpec((tm, d), lambda i: (i, 0)),
        out_shape=jax.ShapeDtypeStruct((m, d), F32),
        compiler_params=_params("parallel"),
        name="merge",
    )(x2, osb, ocv, gates, gates, w_sb, w_cv, w_o, g)


def _mlp_kernel(x_ref, g1_ref, wu_ref, wd_ref, g2_ref, o_ref, h_ref):
    f = pl.program_id(1)

    def down_of_up(h):
        t = jnp.dot(h, wu_ref[...], preferred_element_type=F32)
        t = jnp.square(jnp.maximum(t, 0.0)).astype(BF16)
        return jnp.dot(t, wd_ref[...], preferred_element_type=F32)

    @pl.when(f == 0)
    def _():
        x = x_ref[...]
        h = (x * _rms_scale(x) * g1_ref[...]).astype(BF16)
        h_ref[...] = h
        o_ref[...] = down_of_up(h)

    last = pl.num_programs(1) - 1

    @pl.when(jnp.logical_and(f > 0, f < last))
    def _():
        o_ref[...] += down_of_up(h_ref[...])

    @pl.when(f == last)
    def _():
        y = o_ref[...] + down_of_up(h_ref[...])
        o_ref[...] = x_ref[...] + y * _rms_scale(y) * g2_ref[...]


def _mlp(x2, g1, w_up, w_down, g2):
    m, d = x2.shape
    dff = w_up.shape[1]
    tm, tf = MLP_TM, MLP_TF
    assert dff // tf >= 2
    vec = pl.BlockSpec((1, d), lambda i, f: (0, 0))
    return pl.pallas_call(
        _mlp_kernel,
        grid=(m // tm, dff // tf),
        in_specs=[
            pl.BlockSpec((tm, d), lambda i, f: (i, 0)),
            vec,
            pl.BlockSpec((d, tf), lambda i, f: (0, f)),
            pl.BlockSpec((tf, d), lambda i, f: (f, 0)),
            vec,
        ],
        out_specs=pl.BlockSpec((tm, d), lambda i, f: (i, 0)),
        out_shape=jax.ShapeDtypeStruct((m, d), F32),
        scratch_shapes=[pltpu.VMEM((tm, d), BF16)],
        compiler_params=_params("parallel", "arbitrary"),
        name="mlp",
    )(x2, g1, w_up, w_down, g2)


def kernel(x, g_pre_mix, w_in, b_in, w_dw, b_dw, g_conv_ln, b_conv_ln, w_sb_out, w_conv_out, w_o,
           g_post_mix, g_pre_mlp, w_up, w_down, g_post_mlp):
    b, s, d = x.shape
    d_sb = w_sb_out.shape[1]
    d_conv = w_dw.shape[2]
    n_heads = d_sb // HEAD_DIM
    row = lambda v: v.reshape(1, -1)
    x2 = x.reshape(b * s, d)
    for l in range(w_in.shape[0]):
        qkv, u, gates = _inproj(x2, row(g_pre_mix[l]), w_in[l].astype(BF16), row(b_in[l]))
        o_sb = _attention(qkv.reshape(b, s, 3 * d_sb), n_heads)
        o_cv, (w_sb, w_cv, w_out, w_u, w_d) = _conv_branch(
            u.reshape(b, s, d_conv), w_dw[l], row(b_dw[l]), row(g_conv_ln[l]), row(b_conv_ln[l]),
            (w_sb_out[l], w_conv_out[l], w_o[l], w_up[l], w_down[l]))
        x2 = _merge(x2, o_sb.reshape(b * s, d_sb), o_cv.reshape(b * s, d_conv), gates,
                    w_sb, w_cv, w_out, row(g_post_mix[l]))
        x2 = _mlp(x2, row(g_pre_mlp[l]), w_u, w_d, row(g_post_mlp[l]))
    return x2.reshape(b, s, d)
```
